```python
import numpy as np
import jax, jax.numpy as jnp
from jax import lax

D_MODEL = 1024
BATCH = 32
SEQ = 2048
DEPTH = 2

D_HEAD = 64
ATT_DIM = 3 * D_MODEL // 8
ATT_HEADS = ATT_DIM // D_HEAD
GLA_DV = 64
GLA_DK = 32
GLA_VDIM = 3 * D_MODEL // 8
GLA_HEADS = GLA_VDIM // GLA_DV
GLA_KDIM = GLA_HEADS * GLA_DK
GLA_RANK = 16
GLA_TAU = 16.0
GLA_CHUNK = 64
SG_DIM = D_MODEL - ATT_DIM - GLA_VDIM
SG_GROUP_DIM = 64
SG_GROUPS = SG_DIM // SG_GROUP_DIM
SG_CHUNK = 128
MIX_DIM = ATT_DIM + GLA_VDIM + SG_DIM
DILATIONS = ((128, 1), (512, 4), (2048, 16))
ATT_BLOCK = 128
ROPE_THETA = 500000.0
ROPE_DIM = D_HEAD // 4
D_FF = 4 * D_MODEL
EPS = 1e-6
MASK_VALUE = -1e30
IN_SPLITS = (ATT_DIM, ATT_DIM, ATT_DIM, GLA_KDIM, GLA_KDIM, GLA_VDIM, GLA_VDIM, GLA_RANK, 2 * SG_DIM)
IN_DIM = sum(IN_SPLITS)

kernel_name = "hymba_style_gla_gmlp_dilated_hybrid"


def rms_norm(x, g):
    xf = x.astype(jnp.float32)
    y = xf * lax.rsqrt(jnp.mean(xf * xf, axis=-1, keepdims=True) + EPS)
    return (y * g.astype(jnp.float32)).astype(x.dtype)


def layer_norm(x, g, b):
    xf = x.astype(jnp.float32)
    mu = jnp.mean(xf, axis=-1, keepdims=True)
    xc = xf - mu
    y = xc * lax.rsqrt(jnp.mean(xc * xc, axis=-1, keepdims=True) + EPS)
    return (y * g.astype(jnp.float32) + b.astype(jnp.float32)).astype(x.dtype)


def rope_tables(seq):
    inv = ROPE_THETA ** (-jnp.arange(0, ROPE_DIM, 2, dtype=jnp.float32) / ROPE_DIM)
    ang = jnp.arange(seq, dtype=jnp.float32)[:, None] * inv[None, :]
    return jnp.cos(ang), jnp.sin(ang)


def partial_rope(x, cos, sin):
    half = ROPE_DIM // 2
    c = cos[None, :, None, :].astype(x.dtype)
    s = sin[None, :, None, :].astype(x.dtype)
    x1, x2, rest = x[..., :half], x[..., half:ROPE_DIM], x[..., ROPE_DIM:]
    return jnp.concatenate([x1 * c - x2 * s, x2 * c + x1 * s, rest], axis=-1)


def dilated_branch(q, k, v, window, dilation):
    B, S, H, Dh = q.shape
    n = S // dilation
    nb = -(-n // ATT_BLOCK)
    pad = nb * ATT_BLOCK - n
    span = window // dilation

    def to_sub(t):
        t = t.reshape(B, n, dilation, H, Dh).transpose(0, 2, 3, 1, 4)
        t = jnp.pad(t, ((0, 0), (0, 0), (0, 0), (0, pad), (0, 0)))
        return t.reshape(B, dilation, H, nb, ATT_BLOCK, Dh)

    def with_prev(t):
        prev = jnp.pad(t, ((0, 0), (0, 0), (0, 0), (1, 0), (0, 0), (0, 0)))[:, :, :, :-1]
        return jnp.concatenate([prev, t], axis=4)

    qs = to_sub(q)
    kb = with_prev(to_sub(k))
    vb = with_prev(to_sub(v))
    s = jnp.einsum('bdhnqc,bdhnkc->bdhnqk', qs, kb).astype(jnp.float32)
    qi = jnp.arange(ATT_BLOCK)[:, None] + ATT_BLOCK
    kj = jnp.arange(2 * ATT_BLOCK)[None, :]
    dist = qi - kj
    kpos = jnp.arange(nb)[:, None, None] * ATT_BLOCK + kj[None] - ATT_BLOCK
    valid = (dist >= 0) & (dist <= span) & (kpos >= 0)
    s = jnp.where(valid, s, MASK_VALUE)
    lse = jax.nn.logsumexp(s, axis=-1)
    p = jnp.exp(s - lse[..., None]).astype(v.dtype)
    o = jnp.einsum('bdhnqk,bdhnkc->bdhnqc', p, vb)
    o = o.reshape(B, dilation, H, nb * ATT_BLOCK, Dh)[:, :, :, :n]
    o = o.transpose(0, 3, 1, 2, 4).reshape(B, S, H, Dh)
    lse = lse.reshape(B, dilation, H, nb * ATT_BLOCK)[:, :, :, :n]
    lse = lse.transpose(0, 3, 1, 2).reshape(B, S, H)
    return o, lse


def dilated_attention(q, k, v, cos, sin, q_norm_g, k_norm_g):
    q = partial_rope(rms_norm(q, q_norm_g), cos, sin) * (D_HEAD ** -0.5)
    k = partial_rope(rms_norm(k, k_norm_g), cos, sin)
    outs, lses = [], []
    for window, dilation in DILATIONS:
        o, l = dilated_branch(q, k, v, window, dilation)
        outs.append(o)
        lses.append(l)
    wts = jax.nn.softmax(jnp.stack(lses, axis=0), axis=0).astype(v.dtype)
    o = jnp.einsum('rbsh,rbshc->bshc', wts, jnp.stack(outs, axis=0))
    B, S = o.shape[:2]
    return o.reshape(B, S, ATT_DIM)


def gated_linear_attention(q, k, v, r, a_lr, w_a2, b_a, g_norm):
    B, S = q.shape[:2]
    C = GLA_CHUNK
    N = S // C
    f32 = jnp.float32
    gk = jax.nn.log_sigmoid((a_lr @ w_a2 + b_a).astype(f32)) / GLA_TAU
    bcum = jnp.cumsum(gk.reshape(B, N, C, GLA_HEADS, GLA_DK), axis=2)
    qf = q.astype(f32).reshape(B, N, C, GLA_HEADS, GLA_DK) * (GLA_DK ** -0.5)
    kf = k.astype(f32).reshape(B, N, C, GLA_HEADS, GLA_DK)
    vf = v.astype(f32).reshape(B, N, C, GLA_HEADS, GLA_DV)
    b_last = bcum[:, :, -1:]
    q_t = qf * jnp.exp(bcum)
    k_t = kf * jnp.exp(-bcum)
    k_end = kf * jnp.exp(b_last - bcum)
    causal = jnp.tril(jnp.ones((C, C), dtype=bool))
    A = jnp.where(causal, jnp.einsum('bnthk,bnshk->bnhts', q_t, k_t), 0.0)
    o_intra = jnp.einsum('bnhts,bnshv->bnthv', A, vf)
    kv = jnp.einsum('bnshk,bnshv->nbhkv', k_end, vf)
    decay = jnp.exp(b_last[:, :, 0]).transpose(1, 0, 2, 3)

    def step(state, inp):
        dec, kv_n = inp
        return state * dec[..., None] + kv_n, state

    _, states = lax.scan(step, jnp.zeros((B, GLA_HEADS, GLA_DK, GLA_DV), f32), (decay, kv))
    o_inter = jnp.einsum('bnthk,nbhkv->bnthv', q_t, states)
    o = (o_intra + o_inter).reshape(B, S, GLA_HEADS, GLA_DV)
    o = rms_norm(o, g_norm) * jax.nn.silu(r.astype(f32))
    return o.reshape(B, S, GLA_VDIM).astype(q.dtype)


def spatial_gating(z, ln_g, ln_b, w_s, b_s):
    B, S, _ = z.shape
    N = S // SG_CHUNK
    z = jax.nn.gelu(z)
    u, v = jnp.split(z, 2, axis=-1)
    v = layer_norm(v, ln_g, ln_b).reshape(B, N, SG_CHUNK, SG_GROUPS, SG_GROUP_DIM)
    causal = jnp.tril(jnp.ones((SG_CHUNK, SG_CHUNK), dtype=bool))
    w = jnp.where(causal, w_s, jnp.zeros_like(w_s))
    s = jnp.einsum('gts,bnsgc->bntgc', w, v) + b_s.T[None, None, :, :, None]
    return u * s.reshape(B, S, SG_DIM)


def hybrid_mixer(h, cos, sin, w_in, q_norm_g, k_norm_g, gla_w_a2, gla_b_a, gla_norm_g,
                 sg_ln_g, sg_ln_b, sg_w, sg_b, w_out):
    B, S, _ = h.shape
    proj = h @ w_in
    split_points = np.cumsum(IN_SPLITS)[:-1].tolist()
    q_a, k_a, v_a, q_g, k_g, v_g, r_g, a_g, z_s = jnp.split(proj, split_points, axis=-1)
    o_att = dilated_attention(q_a.reshape(B, S, ATT_HEADS, D_HEAD), k_a.reshape(B, S, ATT_HEADS, D_HEAD),
                              v_a.reshape(B, S, ATT_HEADS, D_HEAD), cos, sin, q_norm_g, k_norm_g)
    o_gla = gated_linear_attention(q_g, k_g, v_g.reshape(B, S, GLA_HEADS, GLA_DV),
                                   r_g.reshape(B, S, GLA_HEADS, GLA_DV), a_g, gla_w_a2, gla_b_a, gla_norm_g)
    o_sg = spatial_gating(z_s, sg_ln_g, sg_ln_b, sg_w, sg_b)
    return jnp.concatenate([o_att, o_gla, o_sg], axis=-1) @ w_out


def squared_relu_mlp(h, w1, w2):
    return jnp.square(jax.nn.relu(h @ w1)) @ w2


def setup_inputs(seed: int = 0) -> dict:
    key = jax.random.key(seed)
    ks = jax.random.split(key, 18)
    nrm = lambda k, shape, scale: jax.random.normal(k, shape, jnp.float32) * scale
    return {
        "x": nrm(ks[0], (BATCH, SEQ, D_MODEL), 1.0),
        "norm1_g": 1.0 + nrm(ks[1], (DEPTH, D_MODEL), 0.02),
        "w_in": nrm(ks[2], (DEPTH, D_MODEL, IN_DIM), D_MODEL ** -0.5),
        "q_norm_g": 1.0 + nrm(ks[3], (DEPTH, D_HEAD), 0.02),
        "k_norm_g": 1.0 + nrm(ks[4], (DEPTH, D_HEAD), 0.02),
        "gla_w_a2": nrm(ks[5], (DEPTH, GLA_RANK, GLA_KDIM), GLA_RANK ** -0.5),
        "gla_b_a": nrm(ks[6], (DEPTH, GLA_KDIM), 0.1),
        "gla_norm_g": 1.0 + nrm(ks[7], (DEPTH, GLA_DV), 0.02),
        "sg_ln_g": 1.0 + nrm(ks[8], (DEPTH, SG_DIM), 0.02),
        "sg_ln_b": nrm(ks[9], (DEPTH, SG_DIM), 0.02),
        "sg_w": nrm(ks[10], (DEPTH, SG_GROUPS, SG_CHUNK, SG_CHUNK), SG_CHUNK ** -0.5),
        "sg_b": 1.0 + nrm(ks[11], (DEPTH, SG_GROUPS, SG_CHUNK), 0.02),
        "w_out": nrm(ks[12], (DEPTH, MIX_DIM, D_MODEL), MIX_DIM ** -0.5),
        "norm2_g": 1.0 + nrm(ks[13], (DEPTH, D_MODEL), 0.02),
        "w_ff1": nrm(ks[14], (DEPTH, D_MODEL, D_FF), D_MODEL ** -0.5),
        "w_ff2": nrm(ks[15], (DEPTH, D_FF, D_MODEL), D_FF ** -0.5),
    }


def reference(x, norm1_g, w_in, q_norm_g, k_norm_g, gla_w_a2, gla_b_a, gla_norm_g,
              sg_ln_g, sg_ln_b, sg_w, sg_b, w_out, norm2_g, w_ff1, w_ff2):
    cos, sin = rope_tables(x.shape[1])
    for l in range(DEPTH):
        h = rms_norm(x, norm1_g[l])
        x = x + hybrid_mixer(h, cos, sin, w_in[l], q_norm_g[l], k_norm_g[l], gla_w_a2[l], gla_b_a[l],
                             gla_norm_g[l], sg_ln_g[l], sg_ln_b[l], sg_w[l], sg_b[l], w_out[l])
        x = x + squared_relu_mlp(rms_norm(x, norm2_g[l]), w_ff1[l], w_ff2[l])
    return x
```

```python
import functools
import math

import jax
import jax.numpy as jnp
from jax import lax
from jax.experimental import pallas as pl
from jax.experimental.pallas import tpu as pltpu

F32 = jnp.float32
BF16 = jnp.bfloat16

LANES = 128
D_HEAD = 64
ATT_BLOCK = 128
ROPE_DIM = D_HEAD // 4
ROPE_THETA = 500000.0
GLA_DK = 32
GLA_DV = 64
GLA_TAU = 16.0
GLA_CHUNK = 64
SG_GROUP_DIM = 64
SG_CHUNK = 128
EPS = 1e-6
MASK_VALUE = -1e30
VMEM_LIMIT_BYTES = 56 * 1024 * 1024


def _dot(a, b):
    return jnp.dot(a, b, preferred_element_type=F32)


def _dot_nt(a, b):
    return lax.dot_general(a, b, (((1,), (1,)), ((), ())), preferred_element_type=F32)


def _dot_tn(a, b):
    return lax.dot_general(a, b, (((0,), (0,)), ((), ())), preferred_element_type=F32)


def _const_spec(shape):
    n = len(shape)
    return pl.BlockSpec(shape, lambda *_: (0,) * n)


def _in_proj_kernel(x_ref, g_ref, w_ref, att_ref, gla_ref, sg_ref, *, n_att, n_gla):
    x = x_ref[...]
    ms = jnp.mean(x * x, axis=-1, keepdims=True)
    h = (x * lax.rsqrt(ms + EPS) * g_ref[...]).astype(BF16)
    att_ref[...] = _dot(h, w_ref[:, :n_att]).astype(BF16)
    gla_ref[...] = _dot(h, w_ref[:, n_att:n_att + n_gla]).astype(BF16)
    sg_ref[...] = _dot(h, w_ref[:, n_att + n_gla:]).astype(BF16)


def _in_proj(x2, g, w, n_att, n_gla, n_sg, tm):
    t, d = x2.shape
    return pl.pallas_call(
        functools.partial(_in_proj_kernel, n_att=n_att, n_gla=n_gla),
        grid=(t // tm,),
        in_specs=[pl.BlockSpec((tm, d), lambda i: (i, 0)),
                  _const_spec((1, d)),
                  _const_spec(w.shape)],
        out_specs=[pl.BlockSpec((tm, n_att), lambda i: (i, 0)),
                   pl.BlockSpec((tm, n_gla), lambda i: (i, 0)),
                   pl.BlockSpec((tm, n_sg), lambda i: (i, 0))],
        out_shape=[jax.ShapeDtypeStruct((t, n_att), BF16),
                   jax.ShapeDtypeStruct((t, n_gla), BF16),
                   jax.ShapeDtypeStruct((t, n_sg), BF16)],
        compiler_params=pltpu.CompilerParams(dimension_semantics=("parallel",),
                                             vmem_limit_bytes=VMEM_LIMIT_BYTES),
    )(x2, g, w)


def _att_kernel(q_ref, k_ref, v_ref, cos_ref, sin_ref, qg_ref, kg_ref, o_ref,
                st_a, st_b, qa_l, qb_l, k_l, va_l, vb_l, o_l, lse_l, *, seq):
    L = ATT_BLOCK
    lane = lax.broadcasted_iota(jnp.int32, (1, LANES), 1)
    head_a = lane < D_HEAD
    hm = lane % D_HEAD
    cos = cos_ref[...]
    sin = sin_ref[...]
    n4 = seq // 4
    n16 = seq // 16

    def norm_rope(x, g):
        x2 = x * x
        sa = jnp.sum(jnp.where(head_a, x2, 0.0), axis=-1, keepdims=True)
        sb = jnp.sum(jnp.where(head_a, 0.0, x2), axis=-1, keepdims=True)
        ms = jnp.where(head_a, sa, sb) * (1.0 / D_HEAD)
        y = x * lax.rsqrt(ms + EPS) * g
        left = pltpu.roll(y, LANES - ROPE_DIM // 2, axis=1)
        right = pltpu.roll(y, ROPE_DIM // 2, axis=1)
        rot = jnp.where(hm < ROPE_DIM // 2, -left, jnp.where(hm < ROPE_DIM, right, 0.0))
        return y * cos + rot * sin

    def regroup(store):
        store(0, 0, st_a[...])
        for rho in range(4):
            blk = st_a[pl.ds(rho, n4, stride=4), :]
            st_b[rho * n4:(rho + 1) * n4, :] = blk
            store(1, rho * n4, blk)
        for rho in range(4):
            for c in range(4):
                blk = st_b[pl.ds(rho * n4 + c, n16, stride=4), :]
                store(2, rho * n4 + c * n16, blk)

    def store_q(li, r0, rows):
        n = rows.shape[0]
        qa_l[li, r0:r0 + n, :] = jnp.where(head_a, rows, 0.0).astype(BF16)
        qb_l[li, r0:r0 + n, :] = jnp.where(head_a, 0.0, rows).astype(BF16)

    def store_k(li, r0, rows):
        k_l[li, r0:r0 + rows.shape[0], :] = rows.astype(BF16)

    def store_v(li, r0, rows):
        n = rows.shape[0]
        ones_a = jnp.broadcast_to(jnp.where(head_a, 1.0, 0.0), (n, LANES))
        va_l[li, r0:r0 + n, :LANES] = jnp.where(head_a, rows, 0.0).astype(BF16)
        va_l[li, r0:r0 + n, LANES:] = ones_a.astype(BF16)
        vb_l[li, r0:r0 + n, :LANES] = jnp.where(head_a, 0.0, rows).astype(BF16)
        vb_l[li, r0:r0 + n, LANES:] = (1.0 - ones_a).astype(BF16)

    st_a[...] = norm_rope(q_ref[0].astype(F32), qg_ref[...]) * (D_HEAD ** -0.5)
    regroup(store_q)
    st_a[...] = norm_rope(k_ref[0].astype(F32), kg_ref[...])
    regroup(store_k)
    st_a[...] = v_ref[0].astype(F32)
    regroup(store_v)

    qi = lax.broadcasted_iota(jnp.int32, (L, 2 * L), 0)
    kj = lax.broadcasted_iota(jnp.int32, (L, 2 * L), 1)
    dist = qi + L - kj
    bias_win = jnp.where((dist >= 0) & (dist <= L), 0.0, MASK_VALUE).astype(F32)
    qi_f = lax.broadcasted_iota(jnp.int32, (L, L), 0)
    kj_f = lax.broadcasted_iota(jnp.int32, (L, L), 1)
    bias_first = jnp.where(qi_f >= kj_f, 0.0, MASK_VALUE).astype(F32)

    def att_block(li, q0, k0, nk, bias):
        qa = qa_l[li, pl.ds(q0, L), :]
        qb = qb_l[li, pl.ds(q0, L), :]
        kk = k_l[li, pl.ds(k0, nk), :]
        sa = _dot_nt(qa, kk) + bias
        sb = _dot_nt(qb, kk) + bias
        ma = jnp.max(sa, axis=-1, keepdims=True)
        mb = jnp.max(sb, axis=-1, keepdims=True)
        pa = jnp.exp(sa - ma).astype(BF16)
        pb = jnp.exp(sb - mb).astype(BF16)
        acc = _dot(pa, va_l[li, pl.ds(k0, nk), :]) + _dot(pb, vb_l[li, pl.ds(k0, nk), :])
        den = acc[:, LANES:]
        o_l[li, pl.ds(q0, L), :] = acc[:, :LANES] / den
        lse_l[li, pl.ds(q0, L), :] = jnp.where(head_a, ma, mb) + jnp.log(den)

    def run_layout(li, n_sub):
        nb = seq // n_sub // L
        for s in range(n_sub):
            base = s * nb * L
            att_block(li, base, base, L, bias_first)
            if nb > 1:
                def body(j, carry, base=base):
                    q0 = pl.multiple_of(base + j * L, L)
                    att_block(li, q0, pl.multiple_of(q0 - L, L), 2 * L, bias_win)
                    return carry
                lax.fori_loop(1, nb, body, 0)

    run_layout(0, 1)
    run_layout(1, 4)
    run_layout(2, 16)

    def ungroup(src_ref):
        for rho in range(4):
            for c in range(4):
                st_b[pl.ds(rho * n4 + c, n16, stride=4), :] = src_ref[2, rho * n4 + c * n16:rho * n4 + (c + 1) * n16, :]
        for rho in range(4):
            st_a[pl.ds(rho, n4, stride=4), :] = st_b[rho * n4:(rho + 1) * n4, :]
        x16 = st_a[...]
        for rho in range(4):
            st_a[pl.ds(rho, n4, stride=4), :] = src_ref[1, rho * n4:(rho + 1) * n4, :]
        x4 = st_a[...]
        return src_ref[0], x4, x16

    l1, l4, l16 = ungroup(lse_l)
    m = jnp.maximum(jnp.maximum(l1, l4), l16)
    w1 = jnp.exp(l1 - m)
    w4 = jnp.exp(l4 - m)
    w16 = jnp.exp(l16 - m)
    inv = 1.0 / (w1 + w4 + w16)
    o1, o4, o16 = ungroup(o_l)
    o_ref[0] = ((w1 * o1 + w4 * o4 + w16 * o16) * inv).astype(BF16)


def _attention(att, cos_t, sin_t, qg, kg, batch, seq):
    n_slab = att.shape[-1] // 3 // LANES
    att3 = att.reshape(batch, seq, att.shape[-1])

    def slab_spec(off):
        return pl.BlockSpec((1, seq, LANES), lambda b, h: (b, 0, off + h))

    return pl.pallas_call(
        functools.partial(_att_kernel, seq=seq),
        grid=(batch, n_slab),
        in_specs=[slab_spec(0), slab_spec(n_slab), slab_spec(2 * n_slab),
                  _const_spec((seq, LANES)), _const_spec((seq, LANES)),
                  _const_spec((1, LANES)), _const_spec((1, LANES))],
        out_specs=pl.BlockSpec((1, seq, LANES), lambda b, h: (b, 0, h)),
        out_shape=jax.ShapeDtypeStruct((batch, seq, n_slab * LANES), BF16),
        scratch_shapes=[pltpu.VMEM((seq, LANES), F32), pltpu.VMEM((seq, LANES), F32),
                        pltpu.VMEM((3, seq, LANES), BF16), pltpu.VMEM((3, seq, LANES), BF16),
                        pltpu.VMEM((3, seq, LANES), BF16),
                        pltpu.VMEM((3, seq, 2 * LANES), BF16), pltpu.VMEM((3, seq, 2 * LANES), BF16),
                        pltpu.VMEM((3, seq, LANES), F32), pltpu.VMEM((3, seq, LANES), F32)],
        compiler_params=pltpu.CompilerParams(dimension_semantics=("parallel", "parallel"),
                                             vmem_limit_bytes=VMEM_LIMIT_BYTES),
    )(att3, att3, att3, cos_t, sin_t, qg, kg)


def _log_sigmoid(x):
    return jnp.minimum(x, 0.0) - jnp.log1p(jnp.exp(-jnp.abs(x)))


def _gla_kernel(x_ref, wa_ref, ba_ref, gn_ref, o_ref, state_ref, *, seq, n_heads, kw, vw, aw):
    C = GLA_CHUNK
    q_off, k_off, v_off, r_off, a_off = 0, kw, 2 * kw, 2 * kw + vw, 2 * kw + 2 * vw
    row = lax.broadcasted_iota(jnp.int32, (C, C), 0)
    col = lax.broadcasted_iota(jnp.int32, (C, C), 1)
    causal = row >= col
    tri = jnp.where(causal, 1.0, 0.0).astype(F32)
    srow = lax.broadcasted_iota(jnp.int32, (vw, kw), 0) // GLA_DV
    scol = lax.broadcasted_iota(jnp.int32, (vw, kw), 1) // GLA_DK
    head_diag = srow == scol
    grow = lax.broadcasted_iota(jnp.int32, (vw, vw), 0) // GLA_DV
    gcol = lax.broadcasted_iota(jnp.int32, (vw, vw), 1) // GLA_DV
    group_sum = jnp.where(grow == gcol, 1.0, 0.0).astype(F32)
    wa = wa_ref[...]
    ba = ba_ref[...]
    gn = gn_ref[...]
    state_ref[...] = jnp.zeros_like(state_ref)

    def chunk(n, carry):
        r0 = pl.multiple_of(n * C, C)
        rows = pl.ds(r0, C)
        q = x_ref[0, rows, q_off:q_off + kw].astype(F32) * (GLA_DK ** -0.5)
        k = x_ref[0, rows, k_off:k_off + kw].astype(F32)
        v = x_ref[0, rows, v_off:v_off + vw]
        r = x_ref[0, rows, r_off:r_off + vw].astype(F32)
        a = x_ref[0, rows, a_off:a_off + aw]
        gk = _log_sigmoid(_dot(a, wa) + ba) / GLA_TAU
        bcum = jnp.dot(tri, gk, precision=lax.Precision.HIGHEST, preferred_element_type=F32)
        b_last = bcum[C - 1:C, :]
        q_t = q * jnp.exp(bcum)
        k_t = k * jnp.exp(-bcum)
        k_end = (k * jnp.exp(b_last - bcum)).astype(BF16)
        decay = jnp.exp(b_last)
        q_tb = q_t.astype(BF16)
        k_tb = k_t.astype(BF16)
        outs = []
        for h in range(n_heads):
            a_h = _dot_nt(q_tb[:, h * GLA_DK:(h + 1) * GLA_DK], k_tb[:, h * GLA_DK:(h + 1) * GLA_DK])
            a_h = jnp.where(causal, a_h, 0.0).astype(BF16)
            outs.append(_dot(a_h, v[:, h * GLA_DV:(h + 1) * GLA_DV]))
        o_intra = jnp.concatenate(outs, axis=1)
        st = state_ref[...]
        o_inter = _dot_nt(q_tb, st.astype(BF16))
        kv_t = _dot_tn(v, k_end)
        state_ref[...] = st * decay + jnp.where(head_diag, kv_t, 0.0)
        o = o_intra + o_inter
        ms = jnp.dot(o * o, group_sum, precision=lax.Precision.HIGHEST, preferred_element_type=F32) * (1.0 / GLA_DV)
        y = o * lax.rsqrt(ms + EPS) * gn
        o_ref[0, rows, :] = (y * (r * jax.nn.sigmoid(r))).astype(BF16)
        return carry

    lax.fori_loop(0, seq // C, chunk, 0)


def _gla(gla, wa, ba, gn, batch, seq, n_heads, kw, vw, aw):
    width = gla.shape[-1]
    gla3 = gla.reshape(batch, seq, width)
    return pl.pallas_call(
        functools.partial(_gla_kernel, seq=seq, n_heads=n_heads, kw=kw, vw=vw, aw=aw),
        grid=(batch,),
        in_specs=[pl.BlockSpec((1, seq, width), lambda b: (b, 0, 0)),
                  _const_spec(wa.shape), _const_spec(ba.shape), _const_spec(gn.shape)],
        out_specs=pl.BlockSpec((1, seq, vw), lambda b: (b, 0, 0)),
        out_shape=jax.ShapeDtypeStruct((batch, seq, vw), BF16),
        scratch_shapes=[pltpu.VMEM((vw, kw), F32)],
        compiler_params=pltpu.CompilerParams(dimension_semantics=("parallel",),
                                             vmem_limit_bytes=VMEM_LIMIT_BYTES),
    )(gla3, wa, ba, gn)


def _sg_kernel(z_ref, lg_ref, lb_ref, w_ref, b_ref, o_ref, *, seq, n_groups, width):
    T = SG_CHUNK
    row = lax.broadcasted_iota(jnp.int32, (T, n_groups * T), 0)
    col = lax.broadcasted_iota(jnp.int32, (T, n_groups * T), 1) % T
    w_cat = jnp.where(row >= col, w_ref[...], 0.0).astype(BF16)
    lane_group = lax.broadcasted_iota(jnp.int32, (1, width), 1) // SG_GROUP_DIM
    lg = lg_ref[...]
    lb = lb_ref[...]
    bias = b_ref[...]
    c0 = math.sqrt(2.0 / math.pi)

    def gelu(x):
        return 0.5 * x * (1.0 + jnp.tanh(c0 * (x + 0.044715 * (x * x * x))))

    def chunk(n, carry):
        rows = pl.ds(pl.multiple_of(n * T, T), T)
        u = gelu(z_ref[0, rows, :width].astype(F32))
        v = gelu(z_ref[0, rows, width:].astype(F32))
        mu = jnp.mean(v, axis=-1, keepdims=True)
        vc = v - mu
        var = jnp.mean(vc * vc, axis=-1, keepdims=True)
        vn = vc * lax.rsqrt(var + EPS) * lg + lb
        stacked = jnp.concatenate([jnp.where(lane_group == g, vn, 0.0) for g in range(n_groups)], axis=0)
        s = _dot(w_cat, stacked.astype(BF16)) + bias
        o_ref[0, rows, :] = (u * s).astype(BF16)
        return carry

    lax.fori_loop(0, seq // T, chunk, 0)


def _spatial_gating(sg, lg, lb, w_cat, bias, batch, seq, n_groups):
    width = sg.shape[-1] // 2
    sg3 = sg.reshape(batch, seq, 2 * width)
    return pl.pallas_call(
        functools.partial(_sg_kernel, seq=seq, n_groups=n_groups, width=width),
        grid=(batch,),
        in_specs=[pl.BlockSpec((1, seq, 2 * width), lambda b: (b, 0, 0)),
                  _const_spec(lg.shape), _const_spec(lb.shape), _const_spec(w_cat.shape), _const_spec(bias.shape)],
        out_specs=pl.BlockSpec((1, seq, width), lambda b: (b, 0, 0)),
        out_shape=jax.ShapeDtypeStruct((batch, seq, width), BF16),
        compiler_params=pltpu.CompilerParams(dimension_semantics=("parallel",),
                                             vmem_limit_bytes=VMEM_LIMIT_BYTES),
    )(sg3, lg, lb, w_cat, bias)


def _out_ffn_kernel(x_ref, oa_ref, og_ref, os_ref, wo_ref, g_ref, w1_ref, w2_ref, y_ref, *, n_a, n_g, ff_chunk):
    mix = (_dot(oa_ref[...], wo_ref[:n_a, :]) + _dot(og_ref[...], wo_ref[n_a:n_a + n_g, :])
           + _dot(os_ref[...], wo_ref[n_a + n_g:, :]))
    x1 = x_ref[...] + mix
    ms = jnp.mean(x1 * x1, axis=-1, keepdims=True)
    h = (x1 * lax.rsqrt(ms + EPS) * g_ref[...]).astype(BF16)
    acc = x1
    d_ff = w1_ref.shape[1]
    for c in range(d_ff // ff_chunk):
        a = jnp.maximum(_dot(h, w1_ref[:, c * ff_chunk:(c + 1) * ff_chunk]), 0.0)
        acc = acc + _dot((a * a).astype(BF16), w2_ref[c * ff_chunk:(c + 1) * ff_chunk, :])
    y_ref[...] = acc


def _out_ffn(x2, oa, og, os_, wo, g2, w1, w2, tm, ff_chunk):
    t, d = x2.shape
    n_a, n_g, n_s = oa.shape[-1], og.shape[-1], os_.shape[-1]
    row_spec = lambda n: pl.BlockSpec((tm, n), lambda i: (i, 0))
    resident = lambda a: pl.BlockSpec(a.shape, lambda i: (0, 0), pipeline_mode=pl.Buffered(1))
    return pl.pallas_call(
        functools.partial(_out_ffn_kernel, n_a=n_a, n_g=n_g, ff_chunk=ff_chunk),
        grid=(t // tm,),
        in_specs=[row_spec(d), row_spec(n_a), row_spec(n_g), row_spec(n_s),
                  resident(wo), _const_spec((1, d)), resident(w1), resident(w2)],
        out_specs=row_spec(d),
        out_shape=jax.ShapeDtypeStruct((t, d), F32),
        compiler_params=pltpu.CompilerParams(dimension_semantics=("parallel",),
                                             vmem_limit_bytes=VMEM_LIMIT_BYTES),
    )(x2, oa.reshape(t, n_a), og.reshape(t, n_g), os_.reshape(t, n_s), wo, g2, w1, w2)


def _pad_cols(w, n):
    return jnp.pad(w, ((0, 0), (0, n - w.shape[1])))


def _rope_tables(seq):
    inv = ROPE_THETA ** (-jnp.arange(0, ROPE_DIM, 2, dtype=F32) / ROPE_DIM)
    ang = jnp.arange(seq, dtype=F32)[:, None] * inv[None, :]
    half = ROPE_DIM // 2
    pad = D_HEAD - ROPE_DIM
    cos_h = jnp.concatenate([jnp.cos(ang), jnp.cos(ang), jnp.ones((seq, pad), F32)], axis=1)
    sin_h = jnp.concatenate([jnp.sin(ang), jnp.sin(ang), jnp.zeros((seq, pad), F32)], axis=1)
    del half
    return jnp.tile(cos_h, (1, LANES // D_HEAD)), jnp.tile(sin_h, (1, LANES // D_HEAD))


def kernel(x, norm1_g, w_in, q_norm_g, k_norm_g, gla_w_a2, gla_b_a, gla_norm_g, sg_ln_g, sg_ln_b, sg_w, sg_b,
           w_out, norm2_g, w_ff1, w_ff2):
    batch, seq, d_model = x.shape
    depth = w_in.shape[0]
    att_dim = 3 * d_model // 8
    gla_vdim = 3 * d_model // 8
    gla_heads = gla_vdim // GLA_DV
    gla_kdim = gla_heads * GLA_DK
    gla_rank = gla_w_a2.shape[1]
    sg_dim = d_model - att_dim - gla_vdim
    sg_groups = sg_dim // SG_GROUP_DIM
    kw = -(-gla_kdim // LANES) * LANES
    aw = -(-gla_rank // LANES) * LANES
    n_att = 3 * att_dim
    n_gla = 2 * kw + 2 * gla_vdim + aw
    n_sg = 2 * sg_dim
    assert att_dim % LANES == 0 and gla_vdim % LANES == 0 and sg_dim % LANES == 0
    assert seq % (16 * ATT_BLOCK) == 0 and seq // 16 == ATT_BLOCK

    cos_t, sin_t = _rope_tables(seq)
    x2 = x.reshape(batch * seq, d_model)
    tm = 512
    splits = [att_dim, att_dim, att_dim, gla_kdim, gla_kdim, gla_vdim, gla_vdim, gla_rank, 2 * sg_dim]
    offs = [0]
    for s in splits:
        offs.append(offs[-1] + s)

    for l in range(depth):
        w = w_in[l]
        seg = [w[:, offs[i]:offs[i + 1]] for i in range(len(splits))]
        w_packed = jnp.concatenate(
            [seg[0], seg[1], seg[2], _pad_cols(seg[3], kw), _pad_cols(seg[4], kw), seg[5], seg[6],
             _pad_cols(seg[7], aw), seg[8]], axis=1).astype(BF16)
        att, gla, sg = _in_proj(x2, norm1_g[l][None, :], w_packed, n_att, n_gla, n_sg, tm)

        qg = jnp.tile(q_norm_g[l], LANES // D_HEAD)[None, :]
        kg = jnp.tile(k_norm_g[l], LANES // D_HEAD)[None, :]
        o_att = _attention(att, cos_t, sin_t, qg, kg, batch, seq)

        wa = jnp.pad(gla_w_a2[l], ((0, aw - gla_rank), (0, kw - gla_kdim))).astype(BF16)
        ba = jnp.pad(gla_b_a[l], (0, kw - gla_kdim))[None, :]
        gn = jnp.tile(gla_norm_g[l], gla_heads)[None, :]
        o_gla = _gla(gla, wa, ba, gn, batch, seq, gla_heads, kw, gla_vdim, aw)

        w_cat = jnp.transpose(sg_w[l], (1, 0, 2)).reshape(SG_CHUNK, sg_groups * SG_CHUNK)
        sg_bias = jnp.repeat(sg_b[l].T, SG_GROUP_DIM, axis=1)
        o_sg = _spatial_gating(sg, sg_ln_g[l][None, :], sg_ln_b[l][None, :], w_cat, sg_bias, batch, seq, sg_groups)

        x2 = _out_ffn(x2, o_att, o_gla, o_sg, w_out[l].astype(BF16), norm2_g[l][None, :],
                      w_ff1[l].astype(BF16), w_ff2[l].astype(BF16), tm, 1024)
    return x2.reshape(batch, seq, d_model)
```

```python
import functools
import math

import jax
import jax.numpy as jnp
from jax import lax
from jax.experimental import pallas as pl
from jax.experimental.pallas import tpu as pltpu

F32 = jnp.float32
BF16 = jnp.bfloat16

LANES = 128
D_HEAD = 64
ATT_BLOCK = 128
ROPE_DIM = D_HEAD // 4
ROPE_THETA = 500000.0
GLA_DK = 32
GLA_DV = 64
GLA_TAU = 16.0
GLA_CHUNK = 64
GLA_ROWS = 256
SG_GROUP_DIM = 64
SG_CHUNK = 128
EPS = 1e-6
MASK_VALUE = -1e30
PREP_ROWS = 512
VMEM_LIMIT_BYTES = 56 * 1024 * 1024


def _dot(a, b):
    return jnp.dot(a, b, preferred_element_type=F32)


def _dot_nt(a, b):
    return lax.dot_general(a, b, (((1,), (1,)), ((), ())), preferred_element_type=F32)


def _dot_tn(a, b):
    return lax.dot_general(a, b, (((0,), (0,)), ((), ())), preferred_element_type=F32)


def _split_dot(x, w):
    hi = x.astype(BF16)
    lo = (x - hi.astype(F32)).astype(BF16)
    return _dot(hi, w) + _dot(lo, w)


def _const_spec(shape):
    n = len(shape)
    return pl.BlockSpec(shape, lambda *_: (0,) * n)


def _in_proj_kernel(x_ref, g_ref, w_ref, att_ref, gla_ref, sg_ref, *, n_att, n_gla):
    x = x_ref[...]
    ms = jnp.mean(x * x, axis=-1, keepdims=True)
    h = (x * lax.rsqrt(ms + EPS) * g_ref[...]).astype(BF16)
    att_ref[...] = _dot(h, w_ref[:, :n_att]).astype(BF16)
    gla_ref[...] = _dot(h, w_ref[:, n_att:n_att + n_gla]).astype(BF16)
    sg_ref[...] = _dot(h, w_ref[:, n_att + n_gla:]).astype(BF16)


def _in_proj(x2, g, w, n_att, n_gla, n_sg, tm):
    t, d = x2.shape
    return pl.pallas_call(
        functools.partial(_in_proj_kernel, n_att=n_att, n_gla=n_gla),
        name="in_proj",
        grid=(t // tm,),
        in_specs=[pl.BlockSpec((tm, d), lambda i: (i, 0)),
                  _const_spec((1, d)),
                  _const_spec(w.shape)],
        out_specs=[pl.BlockSpec((tm, n_att), lambda i: (i, 0)),
                   pl.BlockSpec((tm, n_gla), lambda i: (i, 0)),
                   pl.BlockSpec((tm, n_sg), lambda i: (i, 0))],
        out_shape=[jax.ShapeDtypeStruct((t, n_att), BF16),
                   jax.ShapeDtypeStruct((t, n_gla), BF16),
                   jax.ShapeDtypeStruct((t, n_sg), BF16)],
        compiler_params=pltpu.CompilerParams(dimension_semantics=("parallel",),
                                             vmem_limit_bytes=VMEM_LIMIT_BYTES),
    )(x2, g, w)


def _att_kernel(q_ref, k_ref, v_ref, cos_ref, sin_ref, qg_ref, kg_ref, o_ref,
                st_q, st_k, st_v, mid_q, mid_k, mid_v,
                qa_n, qb_n, k_n, va_n, vb_n, qa_r, qb_r, k_r, va_r, vb_r,
                o_n, lse_n, o_16, lse_16, o_4, lse_4, mid_o, mid_l, st_o, st_l, *, seq):
    L = ATT_BLOCK
    n4 = seq // 4
    n16 = seq // 16
    q4 = L // 4
    half = ROPE_DIM // 2
    lane = lax.broadcasted_iota(jnp.int32, (1, LANES), 1)
    head_a = lane < D_HEAD
    hm = lane % D_HEAD
    ones_a = jnp.where(head_a, 1.0, 0.0).astype(BF16)
    ones_b = jnp.where(head_a, 0.0, 1.0).astype(BF16)

    def split_heads(x):
        return jnp.where(head_a, x, 0.0).astype(BF16), jnp.where(head_a, 0.0, x).astype(BF16)

    gr = lax.broadcasted_iota(jnp.int32, (2 * LANES, 2 * LANES), 0) // D_HEAD
    gc = lax.broadcasted_iota(jnp.int32, (2 * LANES, 2 * LANES), 1) // D_HEAD
    head_sum = jnp.where(gr == gc, 1.0, 0.0).astype(BF16)
    g2 = jnp.concatenate([qg_ref[...] * (D_HEAD ** -0.5), kg_ref[...]], axis=1)

    def rope(y, cos, sin_signed):
        left = pltpu.roll(y, LANES - half, axis=1)
        right = pltpu.roll(y, half, axis=1)
        return y * cos + jnp.where(hm < half, left, right) * sin_signed

    def prep(c, carry):
        rows = pl.ds(pl.multiple_of(c * PREP_ROWS, PREP_ROWS), PREP_ROWS)
        x = jnp.concatenate([q_ref[0, rows, :], k_ref[0, rows, :]], axis=1).astype(F32)
        ms = _split_dot(x * x, head_sum) * (1.0 / D_HEAD)
        y = x * lax.rsqrt(ms + EPS) * g2
        cos = cos_ref[rows, :]
        sin_signed = sin_ref[rows, :]
        yq = rope(y[:, :LANES], cos, sin_signed)
        yk = rope(y[:, LANES:], cos, sin_signed)
        st_q[rows, :] = yq
        st_k[rows, :] = yk
        qa_n[rows, :], qb_n[rows, :] = split_heads(yq)
        k_n[rows, :] = yk.astype(BF16)
        v = v_ref[0, rows, :]
        st_v[rows, :] = v.astype(F32)
        va_n[rows, :LANES] = jnp.where(head_a, v, jnp.zeros_like(v))
        vb_n[rows, :LANES] = jnp.where(head_a, jnp.zeros_like(v), v)
        for ref in (va_n, va_r):
            ref[rows, LANES:] = jnp.broadcast_to(ones_a, (PREP_ROWS, LANES))
        for ref in (vb_n, vb_r):
            ref[rows, LANES:] = jnp.broadcast_to(ones_b, (PREP_ROWS, LANES))
        return carry

    lax.fori_loop(0, seq // PREP_ROWS, prep, 0)

    def regroup(st, mid, emit):
        for rho in range(4):
            mid[rho * n4:(rho + 1) * n4, :] = st[pl.ds(rho, n4, stride=4), :]
        for rho in range(4):
            for c in range(4):
                emit((rho + 4 * c) * n16, mid[pl.ds(rho * n4 + c, n16, stride=4), :])

    def emit_q(r0, blk):
        qa_r[r0:r0 + n16, :], qb_r[r0:r0 + n16, :] = split_heads(blk)

    def emit_k(r0, blk):
        k_r[r0:r0 + n16, :] = blk.astype(BF16)

    def emit_v(r0, blk):
        va_r[r0:r0 + n16, :LANES], vb_r[r0:r0 + n16, :LANES] = split_heads(blk)

    regroup(st_q, mid_q, emit_q)
    regroup(st_k, mid_k, emit_k)
    regroup(st_v, mid_v, emit_v)

    def band_bias(dist):
        return jnp.where((dist >= 0) & (dist <= L), 0.0, MASK_VALUE).astype(F32)

    qi = lax.broadcasted_iota(jnp.int32, (L, L), 0)
    kj = lax.broadcasted_iota(jnp.int32, (L, L), 1)
    qi2 = lax.broadcasted_iota(jnp.int32, (L, 2 * L), 0)
    kj2 = lax.broadcasted_iota(jnp.int32, (L, 2 * L), 1)
    bias_first = band_bias(qi - kj)
    bias_win = band_bias(qi2 + L - kj2)

    def pos4(u):
        return 4 * (u % q4) + u // q4

    bias4_first = band_bias(pos4(qi) - pos4(kj))
    bias4_win = band_bias(pos4(qi2) + L - (pos4(kj2 % L) + L * (kj2 // L)))

    def att_core(qa, qb, kk, va, vb, bias):
        sa = _dot_nt(qa, kk) + bias
        sb = _dot_nt(qb, kk) + bias
        ma = jnp.max(sa, axis=-1, keepdims=True)
        mb = jnp.max(sb, axis=-1, keepdims=True)
        pa = jnp.exp(sa - ma).astype(BF16)
        pb = jnp.exp(sb - mb).astype(BF16)
        acc = _dot(pa, va) + _dot(pb, vb)
        den = acc[:, LANES:]
        return acc[:, :LANES] / den, jnp.where(head_a, ma, mb) + jnp.log(den)

    for j in range(seq // L):
        k0 = max(j - 1, 0) * L
        ks = slice(k0, (j + 1) * L)
        qs = slice(j * L, (j + 1) * L)
        o, lse = att_core(qa_n[qs, :], qb_n[qs, :], k_n[ks, :], va_n[ks, :], vb_n[ks, :],
                          bias_first if j == 0 else bias_win)
        o_n[qs, :] = o
        lse_n[qs, :] = lse

    for r in range(16):
        rs = slice(r * n16, (r + 1) * n16)
        o, lse = att_core(qa_r[rs, :], qb_r[rs, :], k_r[rs, :], va_r[rs, :], vb_r[rs, :], bias_first)
        o_16[rs, :] = o
        lse_16[rs, :] = lse

    def runs(ref, rho, j):
        return [ref[(rho + 4 * c) * n16 + j * q4:(rho + 4 * c) * n16 + (j + 1) * q4, :] for c in range(4)]

    def gather4(ref, rho, j0, j1):
        return jnp.concatenate([blk for j in range(j0, j1 + 1) for blk in runs(ref, rho, j)], axis=0)

    for rho in range(4):
        for j in range(n4 // L):
            j0 = max(j - 1, 0)
            o, lse = att_core(gather4(qa_r, rho, j, j), gather4(qb_r, rho, j, j), gather4(k_r, rho, j0, j),
                              gather4(va_r, rho, j0, j), gather4(vb_r, rho, j0, j),
                              bias4_first if j == 0 else bias4_win)
            for c in range(4):
                dst = slice((rho + 4 * c) * n16 + j * q4, (rho + 4 * c) * n16 + (j + 1) * q4)
                o_4[dst, :] = o[c * q4:(c + 1) * q4, :]
                lse_4[dst, :] = lse[c * q4:(c + 1) * q4, :]

    def mix(la, oa, lb, ob):
        m = jnp.maximum(la, lb)
        wa = jnp.exp(la - m)
        wb = jnp.exp(lb - m)
        den = wa + wb
        return (wa * oa + wb * ob) / den, m + jnp.log(den)

    for c0 in range(0, seq, PREP_ROWS):
        rows = slice(c0, c0 + PREP_ROWS)
        o_16[rows, :], lse_16[rows, :] = mix(lse_4[rows, :], o_4[rows, :], lse_16[rows, :], o_16[rows, :])

    for src, mid, dst in ((o_16, mid_o, st_o), (lse_16, mid_l, st_l)):
        for rho in range(4):
            for c in range(4):
                r0 = (rho + 4 * c) * n16
                mid[pl.ds(rho * n4 + c, n16, stride=4), :] = src[r0:r0 + n16, :]
        for rho in range(4):
            dst[pl.ds(rho, n4, stride=4), :] = mid[rho * n4:(rho + 1) * n4, :]

    for c0 in range(0, seq, PREP_ROWS):
        rows = slice(c0, c0 + PREP_ROWS)
        o, _ = mix(lse_n[rows, :], o_n[rows, :], st_l[rows, :], st_o[rows, :])
        o_ref[0, rows, :] = o.astype(BF16)


def _attention(att, cos_t, sin_t, qg, kg, batch, seq):
    n_slab = att.shape[-1] // 3 // LANES
    att3 = att.reshape(batch, seq, att.shape[-1])

    def slab_spec(off):
        return pl.BlockSpec((1, seq, LANES), lambda b, h: (b, 0, off + h))

    rows_f32 = pltpu.VMEM((seq, LANES), F32)
    rows_bf16 = pltpu.VMEM((seq, LANES), BF16)
    rows2_bf16 = pltpu.VMEM((seq, 2 * LANES), BF16)
    return pl.pallas_call(
        functools.partial(_att_kernel, seq=seq),
        name="dilated_attention",
        grid=(batch, n_slab),
        in_specs=[slab_spec(0), slab_spec(n_slab), slab_spec(2 * n_slab),
                  _const_spec((seq, LANES)), _const_spec((seq, LANES)),
                  _const_spec((1, LANES)), _const_spec((1, LANES))],
        out_specs=pl.BlockSpec((1, seq, LANES), lambda b, h: (b, 0, h)),
        out_shape=jax.ShapeDtypeStruct((batch, seq, n_slab * LANES), BF16),
        scratch_shapes=[rows_f32] * 6
        + [rows_bf16, rows_bf16, rows_bf16, rows2_bf16, rows2_bf16] * 2
        + [rows_f32] * 10,
        compiler_params=pltpu.CompilerParams(dimension_semantics=("parallel", "parallel"),
                                             vmem_limit_bytes=VMEM_LIMIT_BYTES),
    )(att3, att3, att3, cos_t, sin_t, qg, kg)


def _log_sigmoid(x):
    return jnp.minimum(x, 0.0) - jnp.log1p(jnp.exp(-jnp.abs(x)))


def _gla_kernel(x_ref, wa_ref, ba_ref, gn_ref, o_ref, state_ref, *, seq, n_heads, kw, vw, aw):
    C = GLA_CHUNK
    R = GLA_ROWS
    q_off, k_off, v_off, r_off, a_off = 0, kw, 2 * kw, 2 * kw + vw, 2 * kw + 2 * vw
    rr = lax.broadcasted_iota(jnp.int32, (R, R), 0)
    cc = lax.broadcasted_iota(jnp.int32, (R, R), 1)
    intra = (rr // C == cc // C) & (rr >= cc)
    cumsum_mat = jnp.where(intra, 1.0, 0.0).astype(BF16)
    srow = lax.broadcasted_iota(jnp.int32, (vw, kw), 0) // GLA_DV
    scol = lax.broadcasted_iota(jnp.int32, (vw, kw), 1) // GLA_DK
    head_diag = srow == scol
    grow = lax.broadcasted_iota(jnp.int32, (vw, vw), 0) // GLA_DV
    gcol = lax.broadcasted_iota(jnp.int32, (vw, vw), 1) // GLA_DV
    head_sum = jnp.where(grow == gcol, 1.0, 0.0).astype(BF16)
    k_head = lax.broadcasted_iota(jnp.int32, (1, kw), 1) // GLA_DK
    head_a = lax.broadcasted_iota(jnp.int32, (1, LANES), 1) < GLA_DV
    wa = wa_ref[...]
    ba = ba_ref[...]
    gn = gn_ref[...]
    state_ref[...] = jnp.zeros_like(state_ref)

    def block(n, carry):
        rows = pl.ds(pl.multiple_of(n * R, R), R)
        q = x_ref[0, rows, q_off:q_off + kw].astype(F32) * (GLA_DK ** -0.5)
        k = x_ref[0, rows, k_off:k_off + kw].astype(F32)
        v = x_ref[0, rows, v_off:v_off + vw]
        r = x_ref[0, rows, r_off:r_off + vw].astype(F32)
        a = x_ref[0, rows, a_off:a_off + aw]
        gk = _log_sigmoid(_dot(a, wa) + ba) * (1.0 / GLA_TAU)
        bcum = _split_dot_left(cumsum_mat, gk)
        b_last = jnp.concatenate(
            [jnp.broadcast_to(bcum[ci * C + C - 1:ci * C + C, :], (C, kw)) for ci in range(R // C)], axis=0)
        e_pos = jnp.exp(bcum)
        q_t = (q * e_pos).astype(BF16)
        k_t = (k * jnp.exp(-bcum)).astype(BF16)
        k_end = (k * jnp.exp(b_last - bcum)).astype(BF16)

        q_stack = jnp.concatenate([jnp.where(k_head == h, q_t, jnp.zeros_like(q_t)) for h in range(n_heads)], axis=0)
        s_all = _dot_nt(q_stack, k_t)
        slabs = []
        for g in range(vw // LANES):
            vs = v[:, g * LANES:(g + 1) * LANES]
            acc = None
            for h, v_h in ((2 * g, jnp.where(head_a, vs, jnp.zeros_like(vs))),
                           (2 * g + 1, jnp.where(head_a, jnp.zeros_like(vs), vs))):
                s_h = jnp.where(intra, s_all[h * R:(h + 1) * R, :], 0.0).astype(BF16)
                part = _dot(s_h, v_h)
                acc = part if acc is None else acc + part
            slabs.append(acc)
        o_intra = jnp.concatenate(slabs, axis=1)

        st = state_ref[...]
        parts = []
        for ci in range(R // C):
            cs = slice(ci * C, (ci + 1) * C)
            parts.append(_dot_nt(q_t[cs, :], st.astype(BF16)))
            kv_t = _dot_tn(v[cs, :], k_end[cs, :])
            st = st * e_pos[ci * C + C - 1:ci * C + C, :] + jnp.where(head_diag, kv_t, 0.0)
        state_ref[...] = st
        o = o_intra + jnp.concatenate(parts, axis=0)

        ms = _split_dot(o * o, head_sum) * (1.0 / GLA_DV)
        y = o * lax.rsqrt(ms + EPS) * gn
        o_ref[0, rows, :] = (y * (r * jax.nn.sigmoid(r))).astype(BF16)
        return carry

    lax.fori_loop(0, seq // R, block, 0)


def _split_dot_left(w, x):
    hi = x.astype(BF16)
    lo = (x - hi.astype(F32)).astype(BF16)
    return _dot(w, hi) + _dot(w, lo)


def _gla(gla, wa, ba, gn, batch, seq, n_heads, kw, vw, aw):
    width = gla.shape[-1]
    gla3 = gla.reshape(batch, seq, width)
    return pl.pallas_call(
        functools.partial(_gla_kernel, seq=seq, n_heads=n_heads, kw=kw, vw=vw, aw=aw),
        name="gla",
        grid=(batch,),
        in_specs=[pl.BlockSpec((1, seq, width), lambda b: (b, 0, 0)),
                  _const_spec(wa.shape), _const_spec(ba.shape), _const_spec(gn.shape)],
        out_specs=pl.BlockSpec((1, seq, vw), lambda b: (b, 0, 0)),
        out_shape=jax.ShapeDtypeStruct((batch, seq, vw), BF16),
        scratch_shapes=[pltpu.VMEM((vw, kw), F32)],
        compiler_params=pltpu.CompilerParams(dimension_semantics=("parallel",),
                                             vmem_limit_bytes=VMEM_LIMIT_BYTES),
    )(gla3, wa, ba, gn)


def _sg_kernel(z_ref, lg_ref, lb_ref, w_ref, b_ref, o_ref, *, seq, n_groups, width):
    T = SG_CHUNK
    row = lax.broadcasted_iota(jnp.int32, (T, n_groups * T), 0)
    col = lax.broadcasted_iota(jnp.int32, (T, n_groups * T), 1) % T
    w_cat = jnp.where(row >= col, w_ref[...], 0.0).astype(BF16)
    lane_group = lax.broadcasted_iota(jnp.int32, (1, width), 1) // SG_GROUP_DIM
    lg = lg_ref[...]
    lb = lb_ref[...]
    bias = b_ref[...]
    c0 = math.sqrt(2.0 / math.pi)

    def gelu(x):
        return 0.5 * x * (1.0 + jnp.tanh(c0 * (x + 0.044715 * (x * x * x))))

    def chunk(n, carry):
        rows = pl.ds(pl.multiple_of(n * T, T), T)
        u = gelu(z_ref[0, rows, :width].astype(F32))
        v = gelu(z_ref[0, rows, width:].astype(F32))
        mu = jnp.mean(v, axis=-1, keepdims=True)
        vc = v - mu
        var = jnp.mean(vc * vc, axis=-1, keepdims=True)
        vn = vc * lax.rsqrt(var + EPS) * lg + lb
        stacked = jnp.concatenate([jnp.where(lane_group == g, vn, 0.0) for g in range(n_groups)], axis=0)
        s = _dot(w_cat, stacked.astype(BF16)) + bias
        o_ref[0, rows, :] = (u * s).astype(BF16)
        return carry

    lax.fori_loop(0, seq // T, chunk, 0)


def _spatial_gating(sg, lg, lb, w_cat, bias, batch, seq, n_groups):
    width = sg.shape[-1] // 2
    sg3 = sg.reshape(batch, seq, 2 * width)
    return pl.pallas_call(
        functools.partial(_sg_kernel, seq=seq, n_groups=n_groups, width=width),
        name="spatial_gating",
        grid=(batch,),
        in_specs=[pl.BlockSpec((1, seq, 2 * width), lambda b: (b, 0, 0)),
                  _const_spec(lg.shape), _const_spec(lb.shape), _const_spec(w_cat.shape), _const_spec(bias.shape)],
        out_specs=pl.BlockSpec((1, seq, width), lambda b: (b, 0, 0)),
        out_shape=jax.ShapeDtypeStruct((batch, seq, width), BF16),
        compiler_params=pltpu.CompilerParams(dimension_semantics=("parallel",),
                                             vmem_limit_bytes=VMEM_LIMIT_BYTES),
    )(sg3, lg, lb, w_cat, bias)


def _out_ffn_kernel(x_ref, oa_ref, og_ref, os_ref, wo_ref, g_ref, w1_ref, w2_ref, y_ref, *, n_a, n_g, ff_chunk):
    mix = (_dot(oa_ref[...], wo_ref[:n_a, :]) + _dot(og_ref[...], wo_ref[n_a:n_a + n_g, :])
           + _dot(os_ref[...], wo_ref[n_a + n_g:, :]))
    x1 = x_ref[...] + mix
    ms = jnp.mean(x1 * x1, axis=-1, keepdims=True)
    h = (x1 * lax.rsqrt(ms + EPS) * g_ref[...]).astype(BF16)
    acc = x1
    d_ff = w1_ref.shape[1]
    for c in range(d_ff // ff_chunk):
        a = jnp.maximum(_dot(h, w1_ref[:, c * ff_chunk:(c + 1) * ff_chunk]), 0.0)
        acc = acc + _dot((a * a).astype(BF16), w2_ref[c * ff_chunk:(c + 1) * ff_chunk, :])
    y_ref[...] = acc


def _out_ffn(x2, oa, og, os_, wo, g2, w1, w2, tm, ff_chunk):
    t, d = x2.shape
    n_a, n_g, n_s = oa.shape[-1], og.shape[-1], os_.shape[-1]
    row_spec = lambda n: pl.BlockSpec((tm, n), lambda i: (i, 0))
    resident = lambda a: pl.BlockSpec(a.shape, lambda i: (0, 0), pipeline_mode=pl.Buffered(1))
    return pl.pallas_call(
        functools.partial(_out_ffn_kernel, n_a=n_a, n_g=n_g, ff_chunk=ff_chunk),
        name="out_proj_mlp",
        grid=(t // tm,),
        in_specs=[row_spec(d), row_spec(n_a), row_spec(n_g), row_spec(n_s),
                  resident(wo), _const_spec((1, d)), resident(w1), resident(w2)],
        out_specs=row_spec(d),
        out_shape=jax.ShapeDtypeStruct((t, d), F32),
        compiler_params=pltpu.CompilerParams(dimension_semantics=("parallel",),
                                             vmem_limit_bytes=VMEM_LIMIT_BYTES),
    )(x2, oa.reshape(t, n_a), og.reshape(t, n_g), os_.reshape(t, n_s), wo, g2, w1, w2)


def _pad_cols(w, n):
    return jnp.pad(w, ((0, 0), (0, n - w.shape[1])))


def _rope_tables(seq):
    inv = ROPE_THETA ** (-jnp.arange(0, ROPE_DIM, 2, dtype=F32) / ROPE_DIM)
    ang = jnp.arange(seq, dtype=F32)[:, None] * inv[None, :]
    pad = D_HEAD - ROPE_DIM
    cos_h = jnp.concatenate([jnp.cos(ang), jnp.cos(ang), jnp.ones((seq, pad), F32)], axis=1)
    sin_h = jnp.concatenate([-jnp.sin(ang), jnp.sin(ang), jnp.zeros((seq, pad), F32)], axis=1)
    return jnp.tile(cos_h, (1, LANES // D_HEAD)), jnp.tile(sin_h, (1, LANES // D_HEAD))


def kernel(x, norm1_g, w_in, q_norm_g, k_norm_g, gla_w_a2, gla_b_a, gla_norm_g, sg_ln_g, sg_ln_b, sg_w, sg_b,
           w_out, norm2_g, w_ff1, w_ff2):
    batch, seq, d_model = x.shape
    depth = w_in.shape[0]
    att_dim = 3 * d_model // 8
    gla_vdim = 3 * d_model // 8
    gla_heads = gla_vdim // GLA_DV
    gla_kdim = gla_heads * GLA_DK
    gla_rank = gla_w_a2.shape[1]
    sg_dim = d_model - att_dim - gla_vdim
    sg_groups = sg_dim // SG_GROUP_DIM
    kw = -(-gla_kdim // LANES) * LANES
    aw = -(-gla_rank // LANES) * LANES
    n_att = 3 * att_dim
    n_gla = 2 * kw + 2 * gla_vdim + aw
    n_sg = 2 * sg_dim
    assert att_dim % LANES == 0 and gla_vdim % LANES == 0 and sg_dim % LANES == 0
    assert seq // 16 == ATT_BLOCK and seq % GLA_ROWS == 0 and seq % PREP_ROWS == 0

    cos_t, sin_t = _rope_tables(seq)
    x2 = x.reshape(batch * seq, d_model)
    tm = 512
    splits = [att_dim, att_dim, att_dim, gla_kdim, gla_kdim, gla_vdim, gla_vdim, gla_rank, 2 * sg_dim]
    offs = [0]
    for s in splits:
        offs.append(offs[-1] + s)

    for l in range(depth):
        w = w_in[l]
        seg = [w[:, offs[i]:offs[i + 1]] for i in range(len(splits))]
        w_packed = jnp.concatenate(
            [seg[0], seg[1], seg[2], _pad_cols(seg[3], kw), _pad_cols(seg[4], kw), seg[5], seg[6],
             _pad_cols(seg[7], aw), seg[8]], axis=1).astype(BF16)
        att, gla, sg = _in_proj(x2, norm1_g[l][None, :], w_packed, n_att, n_gla, n_sg, tm)

        qg = jnp.tile(q_norm_g[l], LANES // D_HEAD)[None, :]
        kg = jnp.tile(k_norm_g[l], LANES // D_HEAD)[None, :]
        o_att = _attention(att, cos_t, sin_t, qg, kg, batch, seq)

        wa = jnp.pad(gla_w_a2[l], ((0, aw - gla_rank), (0, kw - gla_kdim))).astype(BF16)
        ba = jnp.pad(gla_b_a[l], (0, kw - gla_kdim))[None, :]
        gn = jnp.tile(gla_norm_g[l], gla_heads)[None, :]
        o_gla = _gla(gla, wa, ba, gn, batch, seq, gla_heads, kw, gla_vdim, aw)

        w_cat = jnp.transpose(sg_w[l], (1, 0, 2)).reshape(SG_CHUNK, sg_groups * SG_CHUNK)
        sg_bias = jnp.repeat(sg_b[l].T, SG_GROUP_DIM, axis=1)
        o_sg = _spatial_gating(sg, sg_ln_g[l][None, :], sg_ln_b[l][None, :], w_cat, sg_bias, batch, seq, sg_groups)

        x2 = _out_ffn(x2, o_att, o_gla, o_sg, w_out[l].astype(BF16), norm2_g[l][None, :],
                      w_ff1[l].astype(BF16), w_ff2[l].astype(BF16), tm, 1024)
    return x2.reshape(batch, seq, d_model)
```

```python
import functools
import math

import jax
import jax.numpy as jnp
from jax import lax
from jax.experimental import pallas as pl
from jax.experimental.pallas import tpu as pltpu

F32 = jnp.float32
BF16 = jnp.bfloat16

LANES = 128
D_HEAD = 64
ATT_BLOCK = 128
ROPE_DIM = D_HEAD // 4
ROPE_THETA = 500000.0
GLA_DK = 32
GLA_DV = 64
GLA_TAU = 16.0
GLA_CHUNK = 64
GLA_ROWS = 256
SG_GROUP_DIM = 64
SG_CHUNK = 128
EPS = 1e-6
MASK_VALUE = -1e30
PREP_ROWS = 512
VMEM_LIMIT_BYTES = 56 * 1024 * 1024


def _dot(a, b):
    return jnp.dot(a, b, preferred_element_type=F32)


def _dot_nt(a, b):
    return lax.dot_general(a, b, (((1,), (1,)), ((), ())), preferred_element_type=F32)


def _dot_tn(a, b):
    return lax.dot_general(a, b, (((0,), (0,)), ((), ())), preferred_element_type=F32)


def _split_dot(x, w):
    hi = x.astype(BF16)
    lo = (x - hi.astype(F32)).astype(BF16)
    return _dot(hi, w) + _dot(lo, w)


def _const_spec(shape):
    n = len(shape)
    return pl.BlockSpec(shape, lambda *_: (0,) * n)


def _gelu_tanh(x):
    c0 = math.sqrt(2.0 / math.pi)
    return 0.5 * x * (1.0 + jnp.tanh(c0 * (x + 0.044715 * (x * x * x))))


def _in_proj_kernel(x_ref, g_ref, w_ref, lg_ref, lb_ref, sw_ref, sb_ref, att_ref, gla_ref, sg_ref,
                    *, n_att, n_gla, n_groups):
    T = SG_CHUNK
    width = sg_ref.shape[1]
    x = x_ref[...]
    ms = jnp.mean(x * x, axis=-1, keepdims=True)
    h = (x * lax.rsqrt(ms + EPS) * g_ref[...]).astype(BF16)
    z = _dot(h, w_ref[:, n_att + n_gla:])
    att_ref[...] = _dot(h, w_ref[:, :n_att]).astype(BF16)
    gla_ref[...] = _dot(h, w_ref[:, n_att:n_att + n_gla]).astype(BF16)
    row = lax.broadcasted_iota(jnp.int32, (T, n_groups * T), 0)
    col = lax.broadcasted_iota(jnp.int32, (T, n_groups * T), 1) % T
    w_cat = jnp.where(row >= col, sw_ref[...], 0.0).astype(BF16)
    lane_group = lax.broadcasted_iota(jnp.int32, (1, width), 1) // SG_GROUP_DIM
    for c in range(x.shape[0] // T):
        rows = slice(c * T, (c + 1) * T)
        u = _gelu_tanh(z[rows, :width])
        v = _gelu_tanh(z[rows, width:])
        mu = jnp.mean(v, axis=-1, keepdims=True)
        vc = v - mu
        var = jnp.mean(vc * vc, axis=-1, keepdims=True)
        vn = (vc * lax.rsqrt(var + EPS) * lg_ref[...] + lb_ref[...]).astype(BF16)
        stacked = jnp.concatenate([jnp.where(lane_group == g, vn, jnp.zeros_like(vn)) for g in range(n_groups)],
                                  axis=0)
        s = _dot(w_cat, stacked) + sb_ref[...]
        sg_ref[rows, :] = (u * s).astype(BF16)


def _in_proj(x2, g, w, lg, lb, sw, sb, n_att, n_gla, n_groups, tm):
    t, d = x2.shape
    width = lg.shape[1]
    assert tm % SG_CHUNK == 0
    return pl.pallas_call(
        functools.partial(_in_proj_kernel, n_att=n_att, n_gla=n_gla, n_groups=n_groups),
        name="in_proj",
        grid=(t // tm,),
        in_specs=[pl.BlockSpec((tm, d), lambda i: (i, 0)),
                  _const_spec((1, d)),
                  _const_spec(w.shape),
                  _const_spec(lg.shape), _const_spec(lb.shape), _const_spec(sw.shape), _const_spec(sb.shape)],
        out_specs=[pl.BlockSpec((tm, n_att), lambda i: (i, 0)),
                   pl.BlockSpec((tm, n_gla), lambda i: (i, 0)),
                   pl.BlockSpec((tm, width), lambda i: (i, 0))],
        out_shape=[jax.ShapeDtypeStruct((t, n_att), BF16),
                   jax.ShapeDtypeStruct((t, n_gla), BF16),
                   jax.ShapeDtypeStruct((t, width), BF16)],
        compiler_params=pltpu.CompilerParams(dimension_semantics=("parallel",),
                                             vmem_limit_bytes=VMEM_LIMIT_BYTES),
    )(x2, g, w, lg, lb, sw, sb)


def _att_kernel(q_ref, k_ref, v_ref, cos_ref, sin_ref, qg_ref, kg_ref, o_ref,
                st_q, st_k, st_v, mid_q, mid_k, mid_v,
                qa_n, qb_n, k_n, va_n, vb_n, qa_r, qb_r, k_r, va_r, vb_r,
                o_n, lse_n, o_16, lse_16, o_4, lse_4, mid_o, mid_l, st_o, st_l, *, seq):
    L = ATT_BLOCK
    n4 = seq // 4
    n16 = seq // 16
    q4 = L // 4
    half = ROPE_DIM // 2
    lane = lax.broadcasted_iota(jnp.int32, (1, LANES), 1)
    head_a = lane < D_HEAD
    hm = lane % D_HEAD
    ones_a = jnp.where(head_a, 1.0, 0.0).astype(BF16)
    ones_b = jnp.where(head_a, 0.0, 1.0).astype(BF16)

    def split_heads(x):
        return jnp.where(head_a, x, 0.0).astype(BF16), jnp.where(head_a, 0.0, x).astype(BF16)

    gr = lax.broadcasted_iota(jnp.int32, (2 * LANES, 2 * LANES), 0) // D_HEAD
    gc = lax.broadcasted_iota(jnp.int32, (2 * LANES, 2 * LANES), 1) // D_HEAD
    head_sum = jnp.where(gr == gc, 1.0, 0.0).astype(BF16)
    g2 = jnp.concatenate([qg_ref[...] * (D_HEAD ** -0.5), kg_ref[...]], axis=1)

    def rope(y, cos, sin_signed):
        left = pltpu.roll(y, LANES - half, axis=1)
        right = pltpu.roll(y, half, axis=1)
        return y * cos + jnp.where(hm < half, left, right) * sin_signed

    def prep(c, carry):
        rows = pl.ds(pl.multiple_of(c * PREP_ROWS, PREP_ROWS), PREP_ROWS)
        x = jnp.concatenate([q_ref[0, rows, :], k_ref[0, rows, :]], axis=1).astype(F32)
        ms = _split_dot(x * x, head_sum) * (1.0 / D_HEAD)
        y = x * lax.rsqrt(ms + EPS) * g2
        cos = cos_ref[rows, :]
        sin_signed = sin_ref[rows, :]
        yq = rope(y[:, :LANES], cos, sin_signed)
        yk = rope(y[:, LANES:], cos, sin_signed)
        st_q[rows, :] = yq
        st_k[rows, :] = yk
        qa_n[rows, :], qb_n[rows, :] = split_heads(yq)
        k_n[rows, :] = yk.astype(BF16)
        v = v_ref[0, rows, :]
        st_v[rows, :] = v.astype(F32)
        va_n[rows, :LANES] = jnp.where(head_a, v, jnp.zeros_like(v))
        vb_n[rows, :LANES] = jnp.where(head_a, jnp.zeros_like(v), v)
        for ref in (va_n, va_r):
            ref[rows, LANES:] = jnp.broadcast_to(ones_a, (PREP_ROWS, LANES))
        for ref in (vb_n, vb_r):
            ref[rows, LANES:] = jnp.broadcast_to(ones_b, (PREP_ROWS, LANES))
        return carry

    lax.fori_loop(0, seq // PREP_ROWS, prep, 0)

    def regroup(st, mid, emit):
        for rho in range(4):
            mid[rho * n4:(rho + 1) * n4, :] = st[pl.ds(rho, n4, stride=4), :]
        for rho in range(4):
            for c in range(4):
                emit((rho + 4 * c) * n16, mid[pl.ds(rho * n4 + c, n16, stride=4), :])

    def emit_q(r0, blk):
        qa_r[r0:r0 + n16, :], qb_r[r0:r0 + n16, :] = split_heads(blk)

    def emit_k(r0, blk):
        k_r[r0:r0 + n16, :] = blk.astype(BF16)

    def emit_v(r0, blk):
        va_r[r0:r0 + n16, :LANES], vb_r[r0:r0 + n16, :LANES] = split_heads(blk)

    regroup(st_q, mid_q, emit_q)
    regroup(st_k, mid_k, emit_k)
    regroup(st_v, mid_v, emit_v)

    def band_bias(dist):
        return jnp.where((dist >= 0) & (dist <= L), 0.0, MASK_VALUE).astype(F32)

    qi = lax.broadcasted_iota(jnp.int32, (L, L), 0)
    kj = lax.broadcasted_iota(jnp.int32, (L, L), 1)
    qi2 = lax.broadcasted_iota(jnp.int32, (L, 2 * L), 0)
    kj2 = lax.broadcasted_iota(jnp.int32, (L, 2 * L), 1)
    bias_first = band_bias(qi - kj)
    bias_win = band_bias(qi2 + L - kj2)

    def pos4(u):
        return 4 * (u % q4) + u // q4

    bias4_first = band_bias(pos4(qi) - pos4(kj))
    bias4_win = band_bias(pos4(qi2) + L - (pos4(kj2 % L) + L * (kj2 // L)))

    def att_core(qa, qb, kk, va, vb, bias):
        sa = _dot_nt(qa, kk) + bias
        sb = _dot_nt(qb, kk) + bias
        ma = jnp.max(sa, axis=-1, keepdims=True)
        mb = jnp.max(sb, axis=-1, keepdims=True)
        pa = jnp.exp(sa - ma).astype(BF16)
        pb = jnp.exp(sb - mb).astype(BF16)
        acc = _dot(pa, va) + _dot(pb, vb)
        den = acc[:, LANES:]
        return acc[:, :LANES] / den, jnp.where(head_a, ma, mb) + jnp.log(den)

    for j in range(seq // L):
        k0 = max(j - 1, 0) * L
        ks = slice(k0, (j + 1) * L)
        qs = slice(j * L, (j + 1) * L)
        o, lse = att_core(qa_n[qs, :], qb_n[qs, :], k_n[ks, :], va_n[ks, :], vb_n[ks, :],
                          bias_first if j == 0 else bias_win)
        o_n[qs, :] = o
        lse_n[qs, :] = lse

    for r in range(16):
        rs = slice(r * n16, (r + 1) * n16)
        o, lse = att_core(qa_r[rs, :], qb_r[rs, :], k_r[rs, :], va_r[rs, :], vb_r[rs, :], bias_first)
        o_16[rs, :] = o
        lse_16[rs, :] = lse

    def runs(ref, rho, j):
        return [ref[(rho + 4 * c) * n16 + j * q4:(rho + 4 * c) * n16 + (j + 1) * q4, :] for c in range(4)]

    def gather4(ref, rho, j0, j1):
        return jnp.concatenate([blk for j in range(j0, j1 + 1) for blk in runs(ref, rho, j)], axis=0)

    for rho in range(4):
        for j in range(n4 // L):
            j0 = max(j - 1, 0)
            o, lse = att_core(gather4(qa_r, rho, j, j), gather4(qb_r, rho, j, j), gather4(k_r, rho, j0, j),
                              gather4(va_r, rho, j0, j), gather4(vb_r, rho, j0, j),
                              bias4_first if j == 0 else bias4_win)
            for c in range(4):
                dst = slice((rho + 4 * c) * n16 + j * q4, (rho + 4 * c) * n16 + (j + 1) * q4)
                o_4[dst, :] = o[c * q4:(c + 1) * q4, :]
                lse_4[dst, :] = lse[c * q4:(c + 1) * q4, :]

    def mix(la, oa, lb, ob):
        m = jnp.maximum(la, lb)
        wa = jnp.exp(la - m)
        wb = jnp.exp(lb - m)
        den = wa + wb
        return (wa * oa + wb * ob) / den, m + jnp.log(den)

    for c0 in range(0, seq, PREP_ROWS):
        rows = slice(c0, c0 + PREP_ROWS)
        o_16[rows, :], lse_16[rows, :] = mix(lse_4[rows, :], o_4[rows, :], lse_16[rows, :], o_16[rows, :])

    for src, mid, dst in ((o_16, mid_o, st_o), (lse_16, mid_l, st_l)):
        for rho in range(4):
            for c in range(4):
                r0 = (rho + 4 * c) * n16
                mid[pl.ds(rho * n4 + c, n16, stride=4), :] = src[r0:r0 + n16, :]
        for rho in range(4):
            dst[pl.ds(rho, n4, stride=4), :] = mid[rho * n4:(rho + 1) * n4, :]

    for c0 in range(0, seq, PREP_ROWS):
        rows = slice(c0, c0 + PREP_ROWS)
        o, _ = mix(lse_n[rows, :], o_n[rows, :], st_l[rows, :], st_o[rows, :])
        o_ref[0, rows, :] = o.astype(BF16)


def _attention(att, cos_t, sin_t, qg, kg, batch, seq):
    n_slab = att.shape[-1] // 3 // LANES
    att3 = att.reshape(batch, seq, att.shape[-1])

    def slab_spec(off):
        return pl.BlockSpec((1, seq, LANES), lambda b, h: (b, 0, off + h))

    rows_f32 = pltpu.VMEM((seq, LANES), F32)
    rows_bf16 = pltpu.VMEM((seq, LANES), BF16)
    rows2_bf16 = pltpu.VMEM((seq, 2 * LANES), BF16)
    return pl.pallas_call(
        functools.partial(_att_kernel, seq=seq),
        name="dilated_attention",
        grid=(batch, n_slab),
        in_specs=[slab_spec(0), slab_spec(n_slab), slab_spec(2 * n_slab),
                  _const_spec((seq, LANES)), _const_spec((seq, LANES)),
                  _const_spec((1, LANES)), _const_spec((1, LANES))],
        out_specs=pl.BlockSpec((1, seq, LANES), lambda b, h: (b, 0, h)),
        out_shape=jax.ShapeDtypeStruct((batch, seq, n_slab * LANES), BF16),
        scratch_shapes=[rows_f32] * 6
        + [rows_bf16, rows_bf16, rows_bf16, rows2_bf16, rows2_bf16] * 2
        + [rows_f32] * 10,
        compiler_params=pltpu.CompilerParams(dimension_semantics=("parallel", "parallel"),
                                             vmem_limit_bytes=VMEM_LIMIT_BYTES),
    )(att3, att3, att3, cos_t, sin_t, qg, kg)


def _log_sigmoid(x):
    return jnp.minimum(x, 0.0) - jnp.log(1.0 + jnp.exp(-jnp.abs(x)))


def _gla_kernel(x_ref, wa_ref, ba_ref, gn_ref, o_ref, state_ref, *slots, seq, n_heads, kw, vw, aw):
    slot0, slot1 = slots[:4], slots[4:]
    C = GLA_CHUNK
    R = GLA_ROWS
    q_off, k_off, v_off, r_off, a_off = 0, kw, 2 * kw, 2 * kw + vw, 2 * kw + 2 * vw
    rr = lax.broadcasted_iota(jnp.int32, (R, R), 0)
    cc = lax.broadcasted_iota(jnp.int32, (R, R), 1)
    intra = (rr // C == cc // C) & (rr >= cc)
    cumsum_mat = jnp.where(intra, 1.0, 0.0).astype(BF16)
    heads_per_tile = LANES // GLA_DK
    tile_rows = heads_per_tile * GLA_DV
    tiles = [(t * LANES, t * tile_rows, min((t + 1) * tile_rows, vw)) for t in range(kw // LANES)]
    head_diag = [lax.broadcasted_iota(jnp.int32, (r1 - r0, LANES), 0) // GLA_DV
                 == lax.broadcasted_iota(jnp.int32, (r1 - r0, LANES), 1) // GLA_DK for _, r0, r1 in tiles]
    k_head = lax.broadcasted_iota(jnp.int32, (1, kw), 1) // GLA_DK
    head_a = lax.broadcasted_iota(jnp.int32, (1, LANES), 1) < GLA_DV
    wa = wa_ref[...]
    ba = ba_ref[...]
    gn = gn_ref[...]
    state_ref[...] = jnp.zeros_like(state_ref)
    n_blocks = seq // R

    def prepare(n, slot):
        rows = pl.ds(pl.multiple_of(n * R, R), R)
        qt_ref, kt_ref, ke_ref, decay_ref = slot
        q = x_ref[0, rows, q_off:q_off + kw].astype(F32) * (GLA_DK ** -0.5)
        k = x_ref[0, rows, k_off:k_off + kw].astype(F32)
        a = x_ref[0, rows, a_off:a_off + aw]
        gk = _log_sigmoid(_dot(a, wa) + ba) * (1.0 / GLA_TAU)
        bcum = _split_dot_left(cumsum_mat, gk)
        b_last = jnp.concatenate(
            [jnp.broadcast_to(bcum[ci * C + C - 1:ci * C + C, :], (C, kw)) for ci in range(R // C)], axis=0)
        e_pos = jnp.exp(bcum)
        qt_ref[...] = (q * e_pos).astype(BF16)
        kt_ref[...] = (k * jnp.exp(-bcum)).astype(BF16)
        ke_ref[...] = (k * jnp.exp(b_last - bcum)).astype(BF16)
        for ci in range(R // C):
            decay_ref[ci:ci + 1, :] = e_pos[ci * C + C - 1:ci * C + C, :]

    def consume(n, slot):
        rows = pl.ds(pl.multiple_of(n * R, R), R)
        v = x_ref[0, rows, v_off:v_off + vw]
        r = x_ref[0, rows, r_off:r_off + vw].astype(F32)
        q_t, k_t, k_end, decay = (ref[...] for ref in slot)

        q_stack = jnp.concatenate([jnp.where(k_head == h, q_t, jnp.zeros_like(q_t)) for h in range(n_heads)], axis=0)
        s_all = _dot_nt(q_stack, k_t)
        slabs = []
        for g in range(vw // LANES):
            vs = v[:, g * LANES:(g + 1) * LANES]
            acc = None
            for h, v_h in ((2 * g, jnp.where(head_a, vs, jnp.zeros_like(vs))),
                           (2 * g + 1, jnp.where(head_a, jnp.zeros_like(vs), vs))):
                s_h = s_all[h * R:(h + 1) * R, :].astype(BF16)
                s_h = jnp.where(intra, s_h, jnp.zeros_like(s_h))
                part = _dot(s_h, v_h)
                acc = part if acc is None else acc + part
            slabs.append(acc)
        o_intra = jnp.concatenate(slabs, axis=1)

        st = [state_ref[r0:r1, :] for _, r0, r1 in tiles]
        parts = []
        for ci in range(R // C):
            cs = slice(ci * C, (ci + 1) * C)
            row_parts = []
            for t, (l0, r0, r1) in enumerate(tiles):
                row_parts.append(_dot_nt(q_t[cs, l0:l0 + LANES], st[t].astype(BF16)))
                kv_t = _dot_tn(v[cs, r0:r1], k_end[cs, l0:l0 + LANES])
                st[t] = st[t] * decay[ci:ci + 1, l0:l0 + LANES] + jnp.where(head_diag[t], kv_t, 0.0)
            parts.append(jnp.concatenate(row_parts, axis=1))
        for t, (_, r0, r1) in enumerate(tiles):
            state_ref[r0:r1, :] = st[t]
        o = o_intra + jnp.concatenate(parts, axis=0)

        o2 = o * o
        ms = jnp.concatenate([_pair_sums(o2[:, g * LANES:(g + 1) * LANES], head_a) for g in range(vw // LANES)],
                             axis=1) * (1.0 / GLA_DV)
        y = o * lax.rsqrt(ms + EPS) * gn
        o_ref[0, rows, :] = (y * (r * jax.nn.sigmoid(r))).astype(BF16)

    prepare(0, slot0)

    def pair(m, carry):
        n0 = 2 * m
        prepare(n0 + 1, slot1)
        consume(n0, slot0)
        prepare(jnp.minimum(n0 + 2, n_blocks - 2), slot0)
        consume(n0 + 1, slot1)
        return carry

    lax.fori_loop(0, n_blocks // 2, pair, 0)


def _pair_sums(x, first_half):
    sa = jnp.sum(jnp.where(first_half, x, 0.0), axis=-1, keepdims=True)
    sb = jnp.sum(jnp.where(first_half, 0.0, x), axis=-1, keepdims=True)
    return jnp.where(first_half, sa, sb)


def _split_dot_left(w, x):
    hi = x.astype(BF16)
    lo = (x - hi.astype(F32)).astype(BF16)
    return _dot(w, hi) + _dot(w, lo)


def _gla(gla, wa, ba, gn, batch, seq, n_heads, kw, vw, aw):
    width = gla.shape[-1]
    gla3 = gla.reshape(batch, seq, width)
    return pl.pallas_call(
        functools.partial(_gla_kernel, seq=seq, n_heads=n_heads, kw=kw, vw=vw, aw=aw),
        name="gla",
        grid=(batch,),
        in_specs=[pl.BlockSpec((1, seq, width), lambda b: (b, 0, 0)),
                  _const_spec(wa.shape), _const_spec(ba.shape), _const_spec(gn.shape)],
        out_specs=pl.BlockSpec((1, seq, vw), lambda b: (b, 0, 0)),
        out_shape=jax.ShapeDtypeStruct((batch, seq, vw), BF16),
        scratch_shapes=[pltpu.VMEM((vw, LANES), F32)]
        + ([pltpu.VMEM((GLA_ROWS, kw), BF16)] * 3 + [pltpu.VMEM((8, kw), F32)]) * 2,
        compiler_params=pltpu.CompilerParams(dimension_semantics=("parallel",),
                                             vmem_limit_bytes=VMEM_LIMIT_BYTES),
    )(gla3, wa, ba, gn)


def _out_ffn_kernel(x_ref, oa_ref, og_ref, os_ref, wo_ref, g_ref, w1_ref, w2_ref, y_ref, *, n_a, n_g, ff_chunk):
    mix = (_dot(oa_ref[...], wo_ref[:n_a, :]) + _dot(og_ref[...], wo_ref[n_a:n_a + n_g, :])
           + _dot(os_ref[...], wo_ref[n_a + n_g:, :]))
    x1 = x_ref[...] + mix
    ms = jnp.mean(x1 * x1, axis=-1, keepdims=True)
    h = (x1 * lax.rsqrt(ms + EPS) * g_ref[...]).astype(BF16)
    acc = x1
    d_ff = w1_ref.shape[1]
    for c in range(d_ff // ff_chunk):
        a = jnp.maximum(_dot(h, w1_ref[:, c * ff_chunk:(c + 1) * ff_chunk]), 0.0)
        acc = acc + _dot((a * a).astype(BF16), w2_ref[c * ff_chunk:(c + 1) * ff_chunk, :])
    y_ref[...] = acc


def _out_ffn(x2, oa, og, os_, wo, g2, w1, w2, tm, ff_chunk):
    t, d = x2.shape
    n_a, n_g, n_s = oa.shape[-1], og.shape[-1], os_.shape[-1]
    row_spec = lambda n: pl.BlockSpec((tm, n), lambda i: (i, 0))
    resident = lambda a: pl.BlockSpec(a.shape, lambda i: (0, 0), pipeline_mode=pl.Buffered(1))
    return pl.pallas_call(
        functools.partial(_out_ffn_kernel, n_a=n_a, n_g=n_g, ff_chunk=ff_chunk),
        name="out_proj_mlp",
        grid=(t // tm,),
        in_specs=[row_spec(d), row_spec(n_a), row_spec(n_g), row_spec(n_s),
                  resident(wo), _const_spec((1, d)), resident(w1), resident(w2)],
        out_specs=row_spec(d),
        out_shape=jax.ShapeDtypeStruct((t, d), F32),
        compiler_params=pltpu.CompilerParams(dimension_semantics=("parallel",),
                                             vmem_limit_bytes=VMEM_LIMIT_BYTES),
    )(x2, oa.reshape(t, n_a), og.reshape(t, n_g), os_.reshape(t, n_s), wo, g2, w1, w2)


def _pad_cols(w, n):
    return jnp.pad(w, ((0, 0), (0, n - w.shape[1])))


def _rope_tables(seq):
    inv = ROPE_THETA ** (-jnp.arange(0, ROPE_DIM, 2, dtype=F32) / ROPE_DIM)
    ang = jnp.arange(seq, dtype=F32)[:, None] * inv[None, :]
    pad = D_HEAD - ROPE_DIM
    cos_h = jnp.concatenate([jnp.cos(ang), jnp.cos(ang), jnp.ones((seq, pad), F32)], axis=1)
    sin_h = jnp.concatenate([-jnp.sin(ang), jnp.sin(ang), jnp.zeros((seq, pad), F32)], axis=1)
    return jnp.tile(cos_h, (1, LANES // D_HEAD)), jnp.tile(sin_h, (1, LANES // D_HEAD))


def kernel(x, norm1_g, w_in, q_norm_g, k_norm_g, gla_w_a2, gla_b_a, gla_norm_g, sg_ln_g, sg_ln_b, sg_w, sg_b,
           w_out, norm2_g, w_ff1, w_ff2):
    batch, seq, d_model = x.shape
    depth = w_in.shape[0]
    att_dim = 3 * d_model // 8
    gla_vdim = 3 * d_model // 8
    gla_heads = gla_vdim // GLA_DV
    gla_kdim = gla_heads * GLA_DK
    gla_rank = gla_w_a2.shape[1]
    sg_dim = d_model - att_dim - gla_vdim
    sg_groups = sg_dim // SG_GROUP_DIM
    kw = -(-gla_kdim // LANES) * LANES
    aw = -(-gla_rank // LANES) * LANES
    n_att = 3 * att_dim
    n_gla = 2 * kw + 2 * gla_vdim + aw
    n_sg = 2 * sg_dim
    assert att_dim % LANES == 0 and gla_vdim % LANES == 0 and sg_dim % LANES == 0
    assert seq // 16 == ATT_BLOCK and seq % GLA_ROWS == 0 and seq % PREP_ROWS == 0

    cos_t, sin_t = _rope_tables(seq)
    x2 = x.reshape(batch * seq, d_model)
    tm = 512
    splits = [att_dim, att_dim, att_dim, gla_kdim, gla_kdim, gla_vdim, gla_vdim, gla_rank, 2 * sg_dim]
    offs = [0]
    for s in splits:
        offs.append(offs[-1] + s)

    for l in range(depth):
        w = w_in[l]
        seg = [w[:, offs[i]:offs[i + 1]] for i in range(len(splits))]
        w_packed = jnp.concatenate(
            [seg[0], seg[1], seg[2], _pad_cols(seg[3], kw), _pad_cols(seg[4], kw), seg[5], seg[6],
             _pad_cols(seg[7], aw), seg[8]], axis=1).astype(BF16)
        w_cat = jnp.transpose(sg_w[l], (1, 0, 2)).reshape(SG_CHUNK, sg_groups * SG_CHUNK)
        sg_bias = jnp.repeat(sg_b[l].T, SG_GROUP_DIM, axis=1)
        att, gla, o_sg = _in_proj(x2, norm1_g[l][None, :], w_packed, sg_ln_g[l][None, :], sg_ln_b[l][None, :],
                                  w_cat, sg_bias, n_att, n_gla, sg_groups, tm)

        qg = jnp.tile(q_norm_g[l], LANES // D_HEAD)[None, :]
        kg = jnp.tile(k_norm_g[l], LANES // D_HEAD)[None, :]
        o_att = _attention(att, cos_t, sin_t, qg, kg, batch, seq)

        wa = jnp.pad(gla_w_a2[l], ((0, aw - gla_rank), (0, kw - gla_kdim))).astype(BF16)
        ba = jnp.pad(gla_b_a[l], (0, kw - gla_kdim))[None, :]
        gn = jnp.tile(gla_norm_g[l], gla_heads)[None, :]
        o_gla = _gla(gla, wa, ba, gn, batch, seq, gla_heads, kw, gla_vdim, aw)

        x2 = _out_ffn(x2, o_att, o_gla, o_sg, w_out[l].astype(BF16), norm2_g[l][None, :],
                      w_ff1[l].astype(BF16), w_ff2[l].astype(BF16), tm, 1024)
    return x2.reshape(batch, seq, d_model)
```

```python
import functools
import math

import jax
import jax.numpy as jnp
from jax import lax
from jax.experimental import pallas as pl
from jax.experimental.pallas import tpu as pltpu

F32 = jnp.float32
BF16 = jnp.bfloat16

LANES = 128
D_HEAD = 64
ATT_BLOCK = 128
ROPE_DIM = D_HEAD // 4
ROPE_THETA = 500000.0
GLA_DK = 32
GLA_DV = 64
GLA_TAU = 16.0
GLA_CHUNK = 64
GLA_ROWS = 256
SG_GROUP_DIM = 64
SG_CHUNK = 128
EPS = 1e-6
MASK_VALUE = -1e30
PREP_ROWS = 512
COMBINE_ROWS = 64
VMEM_LIMIT_BYTES =56 * 1024 * 1024


def _dot(a, b):
    return jnp.dot(a, b, preferred_element_type=F32)


def _dot_nt(a, b):
    return lax.dot_general(a, b, (((1,), (1,)), ((), ())), preferred_element_type=F32)


def _dot_tn(a, b):
    return lax.dot_general(a, b, (((0,), (0,)), ((), ())), preferred_element_type=F32)


def _split_dot(x, w):
    hi = x.astype(BF16)
    lo = (x - hi.astype(F32)).astype(BF16)
    return _dot(hi, w) + _dot(lo, w)


def _const_spec(shape):
    n = len(shape)
    return pl.BlockSpec(shape, lambda *_: (0,) * n)


def _gelu_tanh(x):
    c0 = math.sqrt(2.0 / math.pi)
    return 0.5 * x * (1.0 + jnp.tanh(c0 * (x + 0.044715 * (x * x * x))))


def _in_proj_kernel(x_ref, g_ref, w_ref, lg_ref, lb_ref, sw_ref, sb_ref, qkg_ref, cos_ref, sin_ref,
                    att_ref, gla_ref, sg_ref, *, n_att, n_gla, n_groups):
    T = SG_CHUNK
    width = sg_ref.shape[1]
    n_qk = qkg_ref.shape[1]
    half = ROPE_DIM // 2
    x = x_ref[...]
    ms = jnp.mean(x * x, axis=-1, keepdims=True)
    h = (x * lax.rsqrt(ms + EPS) * g_ref[...]).astype(BF16)
    qk_wide = _dot(h, w_ref[:, :n_qk])
    qk = [qk_wide[:, s0:s0 + LANES] for s0 in range(0, n_qk, LANES)]
    z = _dot(h, w_ref[:, n_att + n_gla:])
    att_ref[:, n_qk:] = _dot(h, w_ref[:, n_qk:n_att]).astype(BF16)
    gla_ref[...] = _dot(h, w_ref[:, n_att:n_att + n_gla]).astype(BF16)

    lane = lax.broadcasted_iota(jnp.int32, (1, LANES), 1)
    head_a = lane < D_HEAD
    first_half = lane % D_HEAD < half
    cos = cos_ref[...]
    sin_signed = sin_ref[...]
    for i, xs in enumerate(qk):
        s0 = i * LANES
        y = xs * lax.rsqrt(_pair_sums(xs * xs, head_a) * (1.0 / D_HEAD) + EPS) * qkg_ref[:, s0:s0 + LANES]
        left = pltpu.roll(y, LANES - half, axis=1)
        right = pltpu.roll(y, half, axis=1)
        att_ref[:, s0:s0 + LANES] = (y * cos + jnp.where(first_half, left, right) * sin_signed).astype(BF16)

    row = lax.broadcasted_iota(jnp.int32, (T, n_groups * T), 0)
    col = lax.broadcasted_iota(jnp.int32, (T, n_groups * T), 1) % T
    w_cat = jnp.where(row >= col, sw_ref[...], 0.0).astype(BF16)
    lane_group = lax.broadcasted_iota(jnp.int32, (1, width), 1) // SG_GROUP_DIM
    for c in range(x.shape[0] // T):
        rows = slice(c * T, (c + 1) * T)
        u = _gelu_tanh(z[rows, :width])
        v = _gelu_tanh(z[rows, width:])
        mu = jnp.mean(v, axis=-1, keepdims=True)
        vc = v - mu
        var = jnp.mean(vc * vc, axis=-1, keepdims=True)
        vn = (vc * lax.rsqrt(var + EPS) * lg_ref[...] + lb_ref[...]).astype(BF16)
        stacked = jnp.concatenate([jnp.where(lane_group == g, vn, jnp.zeros_like(vn)) for g in range(n_groups)],
                                  axis=0)
        s = _dot(w_cat, stacked) + sb_ref[...]
        sg_ref[rows, :] = (u * s).astype(BF16)


def _in_proj(x2, g, w, lg, lb, sw, sb, qkg, cos_t, sin_t, n_att, n_gla, n_groups, tm):
    t, d = x2.shape
    width = lg.shape[1]
    seq = cos_t.shape[0]
    assert tm % SG_CHUNK == 0 and seq % tm == 0
    pos_spec = pl.BlockSpec((tm, LANES), lambda i: (i % (seq // tm), 0))
    return pl.pallas_call(
        functools.partial(_in_proj_kernel, n_att=n_att, n_gla=n_gla, n_groups=n_groups),
        name="in_proj",
        grid=(t // tm,),
        in_specs=[pl.BlockSpec((tm, d), lambda i: (i, 0)),
                  _const_spec((1, d)),
                  _const_spec(w.shape),
                  _const_spec(lg.shape), _const_spec(lb.shape), _const_spec(sw.shape), _const_spec(sb.shape),
                  _const_spec(qkg.shape), pos_spec, pos_spec],
        out_specs=[pl.BlockSpec((tm, n_att), lambda i: (i, 0)),
                   pl.BlockSpec((tm, n_gla), lambda i: (i, 0)),
                   pl.BlockSpec((tm, width), lambda i: (i, 0))],
        out_shape=[jax.ShapeDtypeStruct((t, n_att), BF16),
                   jax.ShapeDtypeStruct((t, n_gla), BF16),
                   jax.ShapeDtypeStruct((t, width), BF16)],
        compiler_params=pltpu.CompilerParams(dimension_semantics=("parallel",),
                                             vmem_limit_bytes=VMEM_LIMIT_BYTES),
    )(x2, g, w, lg, lb, sw, sb, qkg, cos_t, sin_t)


def _att_kernel(q_ref, k_ref, v_ref, o_ref,
                st_q, st_k, st_v, mid_q, mid_k, mid_v,
                qa_n, qb_n, va_n, vb_n, qa_r, qb_r, k_r, va_r, vb_r, *res, seq):
    res_n, res_16, res_4, res_mid, res_nat = (res[3 * i:3 * i + 3] for i in range(5))
    L = ATT_BLOCK
    n4 = seq // 4
    n16 = seq // 16
    q4 = L // 4
    lane = lax.broadcasted_iota(jnp.int32, (1, LANES), 1)
    head_a = lane < D_HEAD
    ones_a = jnp.where(head_a, 1.0, 0.0).astype(BF16)
    ones_b = jnp.where(head_a, 0.0, 1.0).astype(BF16)

    def split_heads(x):
        return jnp.where(head_a, x, 0.0).astype(BF16), jnp.where(head_a, 0.0, x).astype(BF16)

    def prep(c, carry):
        rows = pl.ds(pl.multiple_of(c * PREP_ROWS, PREP_ROWS), PREP_ROWS)
        q = q_ref[0, rows, :]
        v = v_ref[0, rows, :]
        zero = jnp.zeros_like(q)
        qa_n[rows, :] = jnp.where(head_a, q, zero)
        qb_n[rows, :] = jnp.where(head_a, zero, q)
        va_n[rows, :LANES] = jnp.where(head_a, v, zero)
        vb_n[rows, :LANES] = jnp.where(head_a, zero, v)
        st_q[rows, :] = q.astype(F32)
        st_k[rows, :] = k_ref[0, rows, :].astype(F32)
        st_v[rows, :] = v.astype(F32)
        for ref in (va_n, va_r):
            ref[rows, LANES:] = jnp.broadcast_to(ones_a, (PREP_ROWS, LANES))
        for ref in (vb_n, vb_r):
            ref[rows, LANES:] = jnp.broadcast_to(ones_b, (PREP_ROWS, LANES))
        return carry

    lax.fori_loop(0, seq // PREP_ROWS, prep, 0)

    def regroup(st, mid, emit):
        for rho in range(4):
            mid[rho * n4:(rho + 1) * n4, :] = st[pl.ds(rho, n4, stride=4), :]
        for rho in range(4):
            for c in range(4):
                emit((rho + 4 * c) * n16, mid[pl.ds(rho * n4 + c, n16, stride=4), :])

    def emit_q(r0, blk):
        qa_r[r0:r0 + n16, :], qb_r[r0:r0 + n16, :] = split_heads(blk)

    def emit_k(r0, blk):
        k_r[r0:r0 + n16, :] = blk.astype(BF16)

    def emit_v(r0, blk):
        va_r[r0:r0 + n16, :LANES], vb_r[r0:r0 + n16, :LANES] = split_heads(blk)

    regroup(st_q, mid_q, emit_q)
    regroup(st_k, mid_k, emit_k)
    regroup(st_v, mid_v, emit_v)

    def band_bias(dist):
        return jnp.where((dist >= 0) & (dist <= L), 0.0, MASK_VALUE).astype(F32)

    qi = lax.broadcasted_iota(jnp.int32, (L, L), 0)
    kj = lax.broadcasted_iota(jnp.int32, (L, L), 1)
    qi2 = lax.broadcasted_iota(jnp.int32, (L, 2 * L), 0)
    kj2 = lax.broadcasted_iota(jnp.int32, (L, 2 * L), 1)
    bias_first = band_bias(qi - kj)
    bias_win = band_bias(qi2 + L - kj2)

    def pos4(u):
        return 4 * (u % q4) + u // q4

    bias4_first = band_bias(pos4(qi) - pos4(kj))
    bias4_win = band_bias(pos4(qi2) + L - (pos4(kj2 % L) + L * (kj2 // L)))

    def att_core(qa, qb, kk, va, vb, bias):
        sa = _dot_nt(qa, kk) + bias
        sb = _dot_nt(qb, kk) + bias
        ma = jnp.max(sa, axis=-1, keepdims=True)
        mb = jnp.max(sb, axis=-1, keepdims=True)
        pa = jnp.exp2(sa - ma).astype(BF16)
        pb = jnp.exp2(sb - mb).astype(BF16)
        acc = _dot(pa, va) + _dot(pb, vb)
        return acc[:, :LANES], acc[:, LANES:], jnp.where(head_a, ma, mb)

    def put(dst, rows, res):
        for ref, val in zip(dst, res):
            ref[rows, :] = val

    for j in range(seq // L):
        k0 = max(j - 1, 0) * L
        ks = slice(k0, (j + 1) * L)
        qs = slice(j * L, (j + 1) * L)
        put(res_n, qs, att_core(qa_n[qs, :], qb_n[qs, :], k_ref[0, ks, :], va_n[ks, :], vb_n[ks, :],
                                bias_first if j == 0 else bias_win))

    for r in range(16):
        rs = slice(r * n16, (r + 1) * n16)
        put(res_16, rs, att_core(qa_r[rs, :], qb_r[rs, :], k_r[rs, :], va_r[rs, :], vb_r[rs, :], bias_first))

    def runs(ref, rho, j):
        return [ref[(rho + 4 * c) * n16 + j * q4:(rho + 4 * c) * n16 + (j + 1) * q4, :] for c in range(4)]

    def gather4(ref, rho, j0, j1):
        return jnp.concatenate([blk for j in range(j0, j1 + 1) for blk in runs(ref, rho, j)], axis=0)

    for rho in range(4):
        for j in range(n4 // L):
            j0 = max(j - 1, 0)
            res = att_core(gather4(qa_r, rho, j, j), gather4(qb_r, rho, j, j), gather4(k_r, rho, j0, j),
                           gather4(va_r, rho, j0, j), gather4(vb_r, rho, j0, j),
                           bias4_first if j == 0 else bias4_win)
            for c in range(4):
                dst = slice((rho + 4 * c) * n16 + j * q4, (rho + 4 * c) * n16 + (j + 1) * q4)
                put(res_4, dst, [val[c * q4:(c + 1) * q4, :] for val in res])

    def combine(a, b):
        (acc_a, den_a, m_a), (acc_b, den_b, m_b) = a, b
        m = jnp.maximum(m_a, m_b)
        wa = jnp.exp2(m_a - m)
        wb = jnp.exp2(m_b - m)
        return wa * acc_a + wb * acc_b, wa * den_a + wb * den_b, m

    def get(src, rows):
        return [ref[rows, :] for ref in src]

    for c0 in range(0, seq, COMBINE_ROWS):
        rows = slice(c0, c0 + COMBINE_ROWS)
        put(res_16, rows, combine(get(res_4, rows), get(res_16, rows)))

    for src, mid, dst in zip(res_16, res_mid, res_nat):
        for rho in range(4):
            for c in range(4):
                r0 = (rho + 4 * c) * n16
                mid[pl.ds(rho * n4 + c, n16, stride=4), :] = src[r0:r0 + n16, :]
        for rho in range(4):
            dst[pl.ds(rho, n4, stride=4), :] = mid[rho * n4:(rho + 1) * n4, :]

    for c0 in range(0, seq, COMBINE_ROWS):
        rows = slice(c0, c0 + COMBINE_ROWS)
        acc, den, _ = combine(get(res_n, rows), get(res_nat, rows))
        o_ref[0, rows, :] = (acc / den).astype(BF16)


def _attention(att, batch, seq):
    n_slab = att.shape[-1] // 3 // LANES
    att3 = att.reshape(batch, seq, att.shape[-1])

    def slab_spec(off):
        return pl.BlockSpec((1, seq, LANES), lambda b, h: (b, 0, off + h))

    rows_f32 = pltpu.VMEM((seq, LANES), F32)
    rows_bf16 = pltpu.VMEM((seq, LANES), BF16)
    rows2_bf16 = pltpu.VMEM((seq, 2 * LANES), BF16)
    return pl.pallas_call(
        functools.partial(_att_kernel, seq=seq),
        name="dilated_attention",
        grid=(batch, n_slab),
        in_specs=[slab_spec(0), slab_spec(n_slab), slab_spec(2 * n_slab)],
        out_specs=pl.BlockSpec((1, seq, LANES), lambda b, h: (b, 0, h)),
        out_shape=jax.ShapeDtypeStruct((batch, seq, n_slab * LANES), BF16),
        scratch_shapes=[rows_f32] * 6
        + [rows_bf16, rows_bf16, rows2_bf16, rows2_bf16]
        + [rows_bf16, rows_bf16, rows_bf16, rows2_bf16, rows2_bf16]
        + [rows_f32] * 15,
        compiler_params=pltpu.CompilerParams(dimension_semantics=("parallel", "parallel"),
                                             vmem_limit_bytes=VMEM_LIMIT_BYTES),
    )(att3, att3, att3)


def _log_sigmoid(x):
    return jnp.minimum(x, 0.0) - jnp.log(1.0 + jnp.exp(-jnp.abs(x)))


def _gla_kernel(x_ref, wa_ref, ba_ref, gn_ref, o_ref, state_ref, *slots, seq, n_heads, kw, vw, aw):
    slot0, slot1 = slots[:4], slots[4:]
    C = GLA_CHUNK
    R = GLA_ROWS
    q_off, k_off, v_off, r_off, a_off = 0, kw, 2 * kw, 2 * kw + vw, 2 * kw + 2 * vw
    rr = lax.broadcasted_iota(jnp.int32, (R, R), 0)
    cc = lax.broadcasted_iota(jnp.int32, (R, R), 1)
    intra = (rr // C == cc // C) & (rr >= cc)
    cumsum_mat = jnp.where(intra, 1.0, 0.0).astype(BF16)
    heads_per_tile = LANES // GLA_DK
    tile_rows = heads_per_tile * GLA_DV
    tiles = [(t * LANES, t * tile_rows, min((t + 1) * tile_rows, vw)) for t in range(kw // LANES)]
    head_diag = [lax.broadcasted_iota(jnp.int32, (r1 - r0, LANES), 0) // GLA_DV
                 == lax.broadcasted_iota(jnp.int32, (r1 - r0, LANES), 1) // GLA_DK for _, r0, r1 in tiles]
    k_head = lax.broadcasted_iota(jnp.int32, (1, kw), 1) // GLA_DK
    head_a = lax.broadcasted_iota(jnp.int32, (1, LANES), 1) < GLA_DV
    wa = wa_ref[...]
    ba = ba_ref[...]
    gn = gn_ref[...]
    state_ref[...] = jnp.zeros_like(state_ref)
    n_blocks = seq // R

    def prepare(n, slot):
        rows = pl.ds(pl.multiple_of(n * R, R), R)
        qt_ref, kt_ref, ke_ref, decay_ref = slot
        q = x_ref[0, rows, q_off:q_off + kw].astype(F32) * (GLA_DK ** -0.5)
        k = x_ref[0, rows, k_off:k_off + kw].astype(F32)
        a = x_ref[0, rows, a_off:a_off + aw]
        gk = _log_sigmoid(_dot(a, wa) + ba) * (1.0 / GLA_TAU)
        bcum = _split_dot_left(cumsum_mat, gk)
        b_last = jnp.concatenate(
            [jnp.broadcast_to(bcum[ci * C + C - 1:ci * C + C, :], (C, kw)) for ci in range(R // C)], axis=0)
        e_pos = jnp.exp(bcum)
        qt_ref[...] = (q * e_pos).astype(BF16)
        kt_ref[...] = (k * jnp.exp(-bcum)).astype(BF16)
        ke_ref[...] = (k * jnp.exp(b_last - bcum)).astype(BF16)
        for ci in range(R // C):
            decay_ref[ci:ci + 1, :] = e_pos[ci * C + C - 1:ci * C + C, :]

    def consume(n, slot):
        rows = pl.ds(pl.multiple_of(n * R, R), R)
        v = x_ref[0, rows, v_off:v_off + vw]
        r = x_ref[0, rows, r_off:r_off + vw].astype(F32)
        q_t, k_t, k_end, decay = (ref[...] for ref in slot)

        q_stack = jnp.concatenate([jnp.where(k_head == h, q_t, jnp.zeros_like(q_t)) for h in range(n_heads)], axis=0)
        s_all = _dot_nt(q_stack, k_t)
        slabs = []
        for g in range(vw // LANES):
            vs = v[:, g * LANES:(g + 1) * LANES]
            acc = None
            for h, v_h in ((2 * g, jnp.where(head_a, vs, jnp.zeros_like(vs))),
                           (2 * g + 1, jnp.where(head_a, jnp.zeros_like(vs), vs))):
                s_h = s_all[h * R:(h + 1) * R, :].astype(BF16)
                s_h = jnp.where(intra, s_h, jnp.zeros_like(s_h))
                part = _dot(s_h, v_h)
                acc = part if acc is None else acc + part
            slabs.append(acc)
        o_intra = jnp.concatenate(slabs, axis=1)

        st = [state_ref[r0:r1, :] for _, r0, r1 in tiles]
        parts = []
        for ci in range(R // C):
            cs = slice(ci * C, (ci + 1) * C)
            row_parts = []
            for t, (l0, r0, r1) in enumerate(tiles):
                row_parts.append(_dot_nt(q_t[cs, l0:l0 + LANES], st[t].astype(BF16)))
                kv_t = _dot_tn(v[cs, r0:r1], k_end[cs, l0:l0 + LANES])
                st[t] = st[t] * decay[ci:ci + 1, l0:l0 + LANES] + jnp.where(head_diag[t], kv_t, 0.0)
            parts.append(jnp.concatenate(row_parts, axis=1))
        for t, (_, r0, r1) in enumerate(tiles):
            state_ref[r0:r1, :] = st[t]
        o = o_intra + jnp.concatenate(parts, axis=0)

        o2 = o * o
        ms = jnp.concatenate([_pair_sums(o2[:, g * LANES:(g + 1) * LANES], head_a) for g in range(vw // LANES)],
                             axis=1) * (1.0 / GLA_DV)
        y = o * lax.rsqrt(ms + EPS) * gn
        o_ref[0, rows, :] = (y * (r * jax.nn.sigmoid(r))).astype(BF16)

    prepare(0, slot0)

    def pair(m, carry):
        n0 = 2 * m
        prepare(n0 + 1, slot1)
        consume(n0, slot0)
        prepare(jnp.minimum(n0 + 2, n_blocks - 2), slot0)
        consume(n0 + 1, slot1)
        return carry

    lax.fori_loop(0, n_blocks // 2, pair, 0)


def _pair_sums(x, first_half):
    sa = jnp.sum(jnp.where(first_half, x, 0.0), axis=-1, keepdims=True)
    sb = jnp.sum(jnp.where(first_half, 0.0, x), axis=-1, keepdims=True)
    return jnp.where(first_half, sa, sb)


def _split_dot_left(w, x):
    hi = x.astype(BF16)
    lo = (x - hi.astype(F32)).astype(BF16)
    return _dot(w, hi) + _dot(w, lo)


def _gla(gla, wa, ba, gn, batch, seq, n_heads, kw, vw, aw):
    width = gla.shape[-1]
    gla3 = gla.reshape(batch, seq, width)
    return pl.pallas_call(
        functools.partial(_gla_kernel, seq=seq, n_heads=n_heads, kw=kw, vw=vw, aw=aw),
        name="gla",
        grid=(batch,),
        in_specs=[pl.BlockSpec((1, seq, width), lambda b: (b, 0, 0)),
                  _const_spec(wa.shape), _const_spec(ba.shape), _const_spec(gn.shape)],
        out_specs=pl.BlockSpec((1, seq, vw), lambda b: (b, 0, 0)),
        out_shape=jax.ShapeDtypeStruct((batch, seq, vw), BF16),
        scratch_shapes=[pltpu.VMEM((vw, LANES), F32)]
        + ([pltpu.VMEM((GLA_ROWS, kw), BF16)] * 3 + [pltpu.VMEM((8, kw), F32)]) * 2,
        compiler_params=pltpu.CompilerParams(dimension_semantics=("parallel",),
                                             vmem_limit_bytes=VMEM_LIMIT_BYTES),
    )(gla3, wa, ba, gn)


def _out_ffn_kernel(x_ref, oa_ref, og_ref, os_ref, wo_ref, g_ref, w1_ref, w2_ref, y_ref, *, n_a, n_g, ff_chunk):
    mix = (_dot(oa_ref[...], wo_ref[:n_a, :]) + _dot(og_ref[...], wo_ref[n_a:n_a + n_g, :])
           + _dot(os_ref[...], wo_ref[n_a + n_g:, :]))
    x1 = x_ref[...] + mix
    ms = jnp.mean(x1 * x1, axis=-1, keepdims=True)
    h = (x1 * lax.rsqrt(ms + EPS) * g_ref[...]).astype(BF16)
    acc = x1
    d_ff = w1_ref.shape[1]
    for c in range(d_ff // ff_chunk):
        a = jnp.maximum(_dot(h, w1_ref[:, c * ff_chunk:(c + 1) * ff_chunk]), 0.0)
        acc = acc + _dot((a * a).astype(BF16), w2_ref[c * ff_chunk:(c + 1) * ff_chunk, :])
    y_ref[...] = acc


def _out_ffn(x2, oa, og, os_, wo, g2, w1, w2, tm, ff_chunk):
    t, d = x2.shape
    n_a, n_g, n_s = oa.shape[-1], og.shape[-1], os_.shape[-1]
    row_spec = lambda n: pl.BlockSpec((tm, n), lambda i: (i, 0))
    resident = lambda a: pl.BlockSpec(a.shape, lambda i: (0, 0), pipeline_mode=pl.Buffered(1))
    return pl.pallas_call(
        functools.partial(_out_ffn_kernel, n_a=n_a, n_g=n_g, ff_chunk=ff_chunk),
        name="out_proj_mlp",
        grid=(t // tm,),
        in_specs=[row_spec(d), row_spec(n_a), row_spec(n_g), row_spec(n_s),
                  resident(wo), _const_spec((1, d)), resident(w1), resident(w2)],
        out_specs=row_spec(d),
        out_shape=jax.ShapeDtypeStruct((t, d), F32),
        compiler_params=pltpu.CompilerParams(dimension_semantics=("parallel",),
                                             vmem_limit_bytes=VMEM_LIMIT_BYTES),
    )(x2, oa.reshape(t, n_a), og.reshape(t, n_g), os_.reshape(t, n_s), wo, g2, w1, w2)


def _pad_cols(w, n):
    return jnp.pad(w, ((0, 0), (0, n - w.shape[1])))


def _rope_tables(seq):
    inv = ROPE_THETA ** (-jnp.arange(0, ROPE_DIM, 2, dtype=F32) / ROPE_DIM)
    ang = jnp.arange(seq, dtype=F32)[:, None] * inv[None, :]
    pad = D_HEAD - ROPE_DIM
    cos_h = jnp.concatenate([jnp.cos(ang), jnp.cos(ang), jnp.ones((seq, pad), F32)], axis=1)
    sin_h = jnp.concatenate([-jnp.sin(ang), jnp.sin(ang), jnp.zeros((seq, pad), F32)], axis=1)
    return jnp.tile(cos_h, (1, LANES // D_HEAD)), jnp.tile(sin_h, (1, LANES // D_HEAD))


def kernel(x, norm1_g, w_in, q_norm_g, k_norm_g, gla_w_a2, gla_b_a, gla_norm_g, sg_ln_g, sg_ln_b, sg_w, sg_b,
           w_out, norm2_g, w_ff1, w_ff2):
    batch, seq, d_model = x.shape
    depth = w_in.shape[0]
    att_dim = 3 * d_model // 8
    gla_vdim = 3 * d_model // 8
    gla_heads = gla_vdim // GLA_DV
    gla_kdim = gla_heads * GLA_DK
    gla_rank = gla_w_a2.shape[1]
    sg_dim = d_model - att_dim - gla_vdim
    sg_groups = sg_dim // SG_GROUP_DIM
    kw = -(-gla_kdim // LANES) * LANES
    aw = -(-gla_rank // LANES) * LANES
    n_att = 3 * att_dim
    n_gla = 2 * kw + 2 * gla_vdim + aw
    n_sg = 2 * sg_dim
    assert att_dim % LANES == 0 and gla_vdim % LANES == 0 and sg_dim % LANES == 0
    assert seq // 16 == ATT_BLOCK and seq % GLA_ROWS == 0 and seq % PREP_ROWS == 0

    cos_t, sin_t = _rope_tables(seq)
    x2 = x.reshape(batch * seq, d_model)
    tm = 512
    splits = [att_dim, att_dim, att_dim, gla_kdim, gla_kdim, gla_vdim, gla_vdim, gla_rank, 2 * sg_dim]
    offs = [0]
    for s in splits:
        offs.append(offs[-1] + s)

    for l in range(depth):
        w = w_in[l]
        seg = [w[:, offs[i]:offs[i + 1]] for i in range(len(splits))]
        w_packed = jnp.concatenate(
            [seg[0], seg[1], seg[2], _pad_cols(seg[3], kw), _pad_cols(seg[4], kw), seg[5], seg[6],
             _pad_cols(seg[7], aw), seg[8]], axis=1).astype(BF16)
        w_cat = jnp.transpose(sg_w[l], (1, 0, 2)).reshape(SG_CHUNK, sg_groups * SG_CHUNK)
        sg_bias = jnp.repeat(sg_b[l].T, SG_GROUP_DIM, axis=1)
        att_heads = att_dim // D_HEAD
        qkg = jnp.concatenate([jnp.tile(q_norm_g[l], att_heads) * (D_HEAD ** -0.5 * math.log2(math.e)),
                               jnp.tile(k_norm_g[l], att_heads)])[None, :]
        att, gla, o_sg = _in_proj(x2, norm1_g[l][None, :], w_packed, sg_ln_g[l][None, :], sg_ln_b[l][None, :],
                                  w_cat, sg_bias, qkg, cos_t, sin_t, n_att, n_gla, sg_groups, tm)

        o_att = _attention(att, batch, seq)

        wa = jnp.pad(gla_w_a2[l], ((0, aw - gla_rank), (0, kw - gla_kdim))).astype(BF16)
        ba = jnp.pad(gla_b_a[l], (0, kw - gla_kdim))[None, :]
        gn = jnp.tile(gla_norm_g[l], gla_heads)[None, :]
        o_gla = _gla(gla, wa, ba, gn, batch, seq, gla_heads, kw, gla_vdim, aw)

        x2 = _out_ffn(x2, o_att, o_gla, o_sg, w_out[l].astype(BF16), norm2_g[l][None, :],
                      w_ff1[l].astype(BF16), w_ff2[l].astype(BF16), tm, 1024)
    return x2.reshape(batch, seq, d_model)
```

```python
import functools
import math

import jax
import jax.numpy as jnp
from jax import lax
from jax.experimental import pallas as pl
from jax.experimental.pallas import tpu as pltpu

F32 = jnp.float32
BF16 = jnp.bfloat16

LANES = 128
D_HEAD = 64
ATT_BLOCK = 128
ROPE_DIM = D_HEAD // 4
ROPE_THETA = 500000.0
GLA_DK = 32
GLA_DV = 64
GLA_TAU = 16.0
GLA_CHUNK = 64
GLA_ROWS = 256
SG_GROUP_DIM = 64
SG_CHUNK = 128
EPS = 1e-6
MASK_VALUE = -1e30
ROW_TILE = 512
FF_CHUNK = 1024
PREP_ROWS = 512
COMBINE_ROWS = 64
VMEM_LIMIT_BYTES = 56 * 1024 * 1024


def _dot(a, b):
    return jnp.dot(a, b, preferred_element_type=F32)


def _dot_nt(a, b):
    return lax.dot_general(a, b, (((1,), (1,)), ((), ())), preferred_element_type=F32)


def _dot_tn(a, b):
    return lax.dot_general(a, b, (((0,), (0,)), ((), ())), preferred_element_type=F32)


def _split_dot_left(w, x):
    hi = x.astype(BF16)
    lo = (x - hi.astype(F32)).astype(BF16)
    return _dot(w, hi) + _dot(w, lo)


def _pair_sums(x, first_half):
    sa = jnp.sum(jnp.where(first_half, x, 0.0), axis=-1, keepdims=True)
    sb = jnp.sum(jnp.where(first_half, 0.0, x), axis=-1, keepdims=True)
    return jnp.where(first_half, sa, sb)


def _const_spec(shape):
    n = len(shape)
    return pl.BlockSpec(shape, lambda *_: (0,) * n)


def _gelu_tanh(x):
    c0 = math.sqrt(2.0 / math.pi)
    return 0.5 * x * (1.0 + jnp.tanh(c0 * (x + 0.044715 * (x * x * x))))


def _in_proj_kernel(x_ref, g_ref, w_ref, lg_ref, lb_ref, sw_ref, sb_ref, qkg_ref, cos_ref, sin_ref,
                    att_ref, gla_ref, sg_ref, *, n_att, n_gla, n_groups):
    T = SG_CHUNK
    width = sg_ref.shape[1]
    n_qk = qkg_ref.shape[1]
    half = ROPE_DIM // 2
    x = x_ref[...]
    ms = jnp.mean(x * x, axis=-1, keepdims=True)
    h = (x * lax.rsqrt(ms + EPS) * g_ref[...]).astype(BF16)
    qk_wide = _dot(h, w_ref[:, :n_qk])
    z = _dot(h, w_ref[:, n_att + n_gla:])
    att_ref[:, n_qk:] = _dot(h, w_ref[:, n_qk:n_att]).astype(BF16)
    gla_ref[...] = _dot(h, w_ref[:, n_att:n_att + n_gla]).astype(BF16)

    lane = lax.broadcasted_iota(jnp.int32, (1, LANES), 1)
    head_a = lane < D_HEAD
    first_half = lane % D_HEAD < half
    cos = cos_ref[...]
    sin_signed = sin_ref[...]
    for s0 in range(0, n_qk, LANES):
        xs = qk_wide[:, s0:s0 + LANES]
        y = xs * lax.rsqrt(_pair_sums(xs * xs, head_a) * (1.0 / D_HEAD) + EPS) * qkg_ref[:, s0:s0 + LANES]
        left = pltpu.roll(y, LANES - half, axis=1)
        right = pltpu.roll(y, half, axis=1)
        att_ref[:, s0:s0 + LANES] = (y * cos + jnp.where(first_half, left, right) * sin_signed).astype(BF16)

    row = lax.broadcasted_iota(jnp.int32, (T, n_groups * T), 0)
    col = lax.broadcasted_iota(jnp.int32, (T, n_groups * T), 1) % T
    w_cat = jnp.where(row >= col, sw_ref[...], 0.0).astype(BF16)
    lane_group = lax.broadcasted_iota(jnp.int32, (1, width), 1) // SG_GROUP_DIM
    for c in range(x.shape[0] // T):
        rows = slice(c * T, (c + 1) * T)
        u = _gelu_tanh(z[rows, :width])
        v = _gelu_tanh(z[rows, width:])
        mu = jnp.mean(v, axis=-1, keepdims=True)
        vc = v - mu
        var = jnp.mean(vc * vc, axis=-1, keepdims=True)
        vn = (vc * lax.rsqrt(var + EPS) * lg_ref[...] + lb_ref[...]).astype(BF16)
        stacked = jnp.concatenate([jnp.where(lane_group == g, vn, jnp.zeros_like(vn)) for g in range(n_groups)],
                                  axis=0)
        s = _dot(w_cat, stacked) + sb_ref[...]
        sg_ref[rows, :] = (u * s).astype(BF16)


def _in_proj(x2, g, w, lg, lb, sw, sb, qkg, cos_t, sin_t, n_att, n_gla, n_groups):
    t, d = x2.shape
    width = lg.shape[1]
    seq = cos_t.shape[0]
    tm = ROW_TILE
    assert tm % SG_CHUNK == 0 and seq % tm == 0
    pos_spec = pl.BlockSpec((tm, LANES), lambda i: (i % (seq // tm), 0))
    consts = (g, w, lg, lb, sw, sb, qkg)
    return pl.pallas_call(
        functools.partial(_in_proj_kernel, n_att=n_att, n_gla=n_gla, n_groups=n_groups),
        name="in_proj",
        grid=(t // tm,),
        in_specs=[pl.BlockSpec((tm, d), lambda i: (i, 0))] + [_const_spec(c.shape) for c in consts]
        + [pos_spec, pos_spec],
        out_specs=[pl.BlockSpec((tm, n_att), lambda i: (i, 0)),
                   pl.BlockSpec((tm, n_gla), lambda i: (i, 0)),
                   pl.BlockSpec((tm, width), lambda i: (i, 0))],
        out_shape=[jax.ShapeDtypeStruct((t, n_att), BF16),
                   jax.ShapeDtypeStruct((t, n_gla), BF16),
                   jax.ShapeDtypeStruct((t, width), BF16)],
        compiler_params=pltpu.CompilerParams(dimension_semantics=("parallel",),
                                             vmem_limit_bytes=VMEM_LIMIT_BYTES),
    )(x2, *consts, cos_t, sin_t)


def _att_kernel(q_ref, k_ref, v_ref, o_ref,
                st_q, st_k, st_v, mid_q, mid_k, mid_v,
                qa_n, qb_n, va_n, vb_n, qa_r, qb_r, k_r, va_r, vb_r, *res, seq):
    res_n, res_16, res_4, res_mid, res_nat = (res[3 * i:3 * i + 3] for i in range(5))
    L = ATT_BLOCK
    n4 = seq // 4
    n16 = seq // 16
    q4 = L // 4
    lane = lax.broadcasted_iota(jnp.int32, (1, LANES), 1)
    head_a = lane < D_HEAD
    ones_a = jnp.where(head_a, 1.0, 0.0).astype(BF16)
    ones_b = jnp.where(head_a, 0.0, 1.0).astype(BF16)

    def split_heads(x):
        return jnp.where(head_a, x, 0.0).astype(BF16), jnp.where(head_a, 0.0, x).astype(BF16)

    def prep(c, carry):
        rows = pl.ds(pl.multiple_of(c * PREP_ROWS, PREP_ROWS), PREP_ROWS)
        q = q_ref[0, rows, :]
        v = v_ref[0, rows, :]
        zero = jnp.zeros_like(q)
        qa_n[rows, :] = jnp.where(head_a, q, zero)
        qb_n[rows, :] = jnp.where(head_a, zero, q)
        va_n[rows, :LANES] = jnp.where(head_a, v, zero)
        vb_n[rows, :LANES] = jnp.where(head_a, zero, v)
        st_q[rows, :] = q.astype(F32)
        st_k[rows, :] = k_ref[0, rows, :].astype(F32)
        st_v[rows, :] = v.astype(F32)
        for ref in (va_n, va_r):
            ref[rows, LANES:] = jnp.broadcast_to(ones_a, (PREP_ROWS, LANES))
        for ref in (vb_n, vb_r):
            ref[rows, LANES:] = jnp.broadcast_to(ones_b, (PREP_ROWS, LANES))
        return carry

    lax.fori_loop(0, seq // PREP_ROWS, prep, 0)

    def regroup(st, mid, emit):
        for rho in range(4):
            mid[rho * n4:(rho + 1) * n4, :] = st[pl.ds(rho, n4, stride=4), :]
        for rho in range(4):
            for c in range(4):
                emit((rho + 4 * c) * n16, mid[pl.ds(rho * n4 + c, n16, stride=4), :])

    def emit_q(r0, blk):
        qa_r[r0:r0 + n16, :], qb_r[r0:r0 + n16, :] = split_heads(blk)

    def emit_k(r0, blk):
        k_r[r0:r0 + n16, :] = blk.astype(BF16)

    def emit_v(r0, blk):
        va_r[r0:r0 + n16, :LANES], vb_r[r0:r0 + n16, :LANES] = split_heads(blk)

    regroup(st_q, mid_q, emit_q)
    regroup(st_k, mid_k, emit_k)
    regroup(st_v, mid_v, emit_v)

    def band_bias(dist):
        return jnp.where((dist >= 0) & (dist <= L), 0.0, MASK_VALUE).astype(F32)

    qi = lax.broadcasted_iota(jnp.int32, (L, L), 0)
    kj = lax.broadcasted_iota(jnp.int32, (L, L), 1)
    qi2 = lax.broadcasted_iota(jnp.int32, (L, 2 * L), 0)
    kj2 = lax.broadcasted_iota(jnp.int32, (L, 2 * L), 1)
    bias_first = band_bias(qi - kj)
    bias_win = band_bias(qi2 + L - kj2)

    def pos4(u):
        return 4 * (u % q4) + u // q4

    bias4_first = band_bias(pos4(qi) - pos4(kj))
    bias4_win = band_bias(pos4(qi2) + L - (pos4(kj2 % L) + L * (kj2 // L)))

    def att_core(qa, qb, kk, va, vb, bias):
        sa = _dot_nt(qa, kk) + bias
        sb = _dot_nt(qb, kk) + bias
        ma = jnp.max(sa, axis=-1, keepdims=True)
        mb = jnp.max(sb, axis=-1, keepdims=True)
        pa = jnp.exp2(sa - ma).astype(BF16)
        pb = jnp.exp2(sb - mb).astype(BF16)
        acc = _dot(pa, va) + _dot(pb, vb)
        return acc[:, :LANES], acc[:, LANES:], jnp.where(head_a, ma, mb)

    def put(dst, rows, res):
        for ref, val in zip(dst, res):
            ref[rows, :] = val

    for j in range(seq // L):
        k0 = max(j - 1, 0) * L
        ks = slice(k0, (j + 1) * L)
        qs = slice(j * L, (j + 1) * L)
        put(res_n, qs, att_core(qa_n[qs, :], qb_n[qs, :], k_ref[0, ks, :], va_n[ks, :], vb_n[ks, :],
                                bias_first if j == 0 else bias_win))

    for r in range(16):
        rs = slice(r * n16, (r + 1) * n16)
        put(res_16, rs, att_core(qa_r[rs, :], qb_r[rs, :], k_r[rs, :], va_r[rs, :], vb_r[rs, :], bias_first))

    def runs(ref, rho, j):
        return [ref[(rho + 4 * c) * n16 + j * q4:(rho + 4 * c) * n16 + (j + 1) * q4, :] for c in range(4)]

    def gather4(ref, rho, j0, j1):
        return jnp.concatenate([blk for j in range(j0, j1 + 1) for blk in runs(ref, rho, j)], axis=0)

    for rho in range(4):
        for j in range(n4 // L):
            j0 = max(j - 1, 0)
            res = att_core(gather4(qa_r, rho, j, j), gather4(qb_r, rho, j, j), gather4(k_r, rho, j0, j),
                           gather4(va_r, rho, j0, j), gather4(vb_r, rho, j0, j),
                           bias4_first if j == 0 else bias4_win)
            for c in range(4):
                dst = slice((rho + 4 * c) * n16 + j * q4, (rho + 4 * c) * n16 + (j + 1) * q4)
                put(res_4, dst, [val[c * q4:(c + 1) * q4, :] for val in res])

    def combine(a, b):
        (acc_a, den_a, m_a), (acc_b, den_b, m_b) = a, b
        m = jnp.maximum(m_a, m_b)
        wa = jnp.exp2(m_a - m)
        wb = jnp.exp2(m_b - m)
        return wa * acc_a + wb * acc_b, wa * den_a + wb * den_b, m

    def get(src, rows):
        return [ref[rows, :] for ref in src]

    for c0 in range(0, seq, COMBINE_ROWS):
        rows = slice(c0, c0 + COMBINE_ROWS)
        put(res_16, rows, combine(get(res_4, rows), get(res_16, rows)))

    for src, mid, dst in zip(res_16, res_mid, res_nat):
        for rho in range(4):
            for c in range(4):
                r0 = (rho + 4 * c) * n16
                mid[pl.ds(rho * n4 + c, n16, stride=4), :] = src[r0:r0 + n16, :]
        for rho in range(4):
            dst[pl.ds(rho, n4, stride=4), :] = mid[rho * n4:(rho + 1) * n4, :]

    for c0 in range(0, seq, COMBINE_ROWS):
        rows = slice(c0, c0 + COMBINE_ROWS)
        acc, den, _ = combine(get(res_n, rows), get(res_nat, rows))
        o_ref[0, rows, :] = (acc / den).astype(BF16)


def _attention(att, batch, seq):
    n_slab = att.shape[-1] // 3 // LANES
    att3 = att.reshape(batch, seq, att.shape[-1])

    def slab_spec(off):
        return pl.BlockSpec((1, seq, LANES), lambda b, h: (b, 0, off + h))

    rows_f32 = pltpu.VMEM((seq, LANES), F32)
    rows_bf16 = pltpu.VMEM((seq, LANES), BF16)
    rows2_bf16 = pltpu.VMEM((seq, 2 * LANES), BF16)
    return pl.pallas_call(
        functools.partial(_att_kernel, seq=seq),
        name="dilated_attention",
        grid=(batch, n_slab),
        in_specs=[slab_spec(0), slab_spec(n_slab), slab_spec(2 * n_slab)],
        out_specs=pl.BlockSpec((1, seq, LANES), lambda b, h: (b, 0, h)),
        out_shape=jax.ShapeDtypeStruct((batch, seq, n_slab * LANES), BF16),
        scratch_shapes=[rows_f32] * 6
        + [rows_bf16, rows_bf16, rows2_bf16, rows2_bf16]
        + [rows_bf16, rows_bf16, rows_bf16, rows2_bf16, rows2_bf16]
        + [rows_f32] * 15,
        compiler_params=pltpu.CompilerParams(dimension_semantics=("parallel", "parallel"),
                                             vmem_limit_bytes=VMEM_LIMIT_BYTES),
    )(att3, att3, att3)


def _log_sigmoid(x):
    return jnp.minimum(x, 0.0) - jnp.log(1.0 + jnp.exp(-jnp.abs(x)))


def _gla_kernel(x_ref, wa_ref, ba_ref, gn_ref, o_ref, state_ref, *slots, seq, n_heads, kw, vw):
    slot0, slot1 = slots[:4], slots[4:]
    C = GLA_CHUNK
    R = GLA_ROWS
    q_off, k_off, v_off, r_off = 0, kw, 2 * kw, 2 * kw + vw
    rr = lax.broadcasted_iota(jnp.int32, (R, R), 0)
    cc = lax.broadcasted_iota(jnp.int32, (R, R), 1)
    intra = (rr // C == cc // C) & (rr >= cc)
    cumsum_mat = jnp.where(intra, 1.0, 0.0).astype(BF16)
    heads_per_tile = LANES // GLA_DK
    tile_rows = heads_per_tile * GLA_DV
    tiles = [(t * LANES, t * tile_rows, min((t + 1) * tile_rows, vw)) for t in range(kw // LANES)]
    head_diag = [lax.broadcasted_iota(jnp.int32, (r1 - r0, LANES), 0) // GLA_DV
                 == lax.broadcasted_iota(jnp.int32, (r1 - r0, LANES), 1) // GLA_DK for _, r0, r1 in tiles]
    k_head = lax.broadcasted_iota(jnp.int32, (1, kw), 1) // GLA_DK
    head_a = lax.broadcasted_iota(jnp.int32, (1, LANES), 1) < GLA_DV
    wa = wa_ref[...]
    ba = ba_ref[...]
    gn = gn_ref[...]
    state_ref[...] = jnp.zeros_like(state_ref)
    n_blocks = seq // R

    def prepare(n, slot):
        rows = pl.ds(pl.multiple_of(n * R, R), R)
        qt_ref, kt_ref, ke_ref, decay_ref = slot
        a = x_ref[0, rows, q_off:q_off + kw]
        q = a.astype(F32) * (GLA_DK ** -0.5)
        k = x_ref[0, rows, k_off:k_off + kw].astype(F32)
        gk = _log_sigmoid(_dot(a, wa) + ba) * (1.0 / GLA_TAU)
        bcum = _split_dot_left(cumsum_mat, gk)
        b_last = jnp.concatenate(
            [jnp.broadcast_to(bcum[ci * C + C - 1:ci * C + C, :], (C, kw)) for ci in range(R // C)], axis=0)
        e_pos = jnp.exp(bcum)
        qt_ref[...] = (q * e_pos).astype(BF16)
        kt_ref[...] = (k * jnp.exp(-bcum)).astype(BF16)
        ke_ref[...] = (k * jnp.exp(b_last - bcum)).astype(BF16)
        for ci in range(R // C):
            decay_ref[ci:ci + 1, :] = e_pos[ci * C + C - 1:ci * C + C, :]

    def consume(n, slot):
        rows = pl.ds(pl.multiple_of(n * R, R), R)
        v = x_ref[0, rows, v_off:v_off + vw]
        r = x_ref[0, rows, r_off:r_off + vw].astype(F32)
        q_t, k_t, k_end, decay = (ref[...] for ref in slot)

        q_stack = jnp.concatenate([jnp.where(k_head == h, q_t, jnp.zeros_like(q_t)) for h in range(n_heads)], axis=0)
        s_all = _dot_nt(q_stack, k_t)
        slabs = []
        for sl in range(vw // LANES):
            vs = v[:, sl * LANES:(sl + 1) * LANES]
            acc = None
            for h, v_h in ((2 * sl, jnp.where(head_a, vs, jnp.zeros_like(vs))),
                           (2 * sl + 1, jnp.where(head_a, jnp.zeros_like(vs), vs))):
                s_h = s_all[h * R:(h + 1) * R, :].astype(BF16)
                s_h = jnp.where(intra, s_h, jnp.zeros_like(s_h))
                part = _dot(s_h, v_h)
                acc = part if acc is None else acc + part
            slabs.append(acc)
        o_intra = jnp.concatenate(slabs, axis=1)

        st = [state_ref[r0:r1, :] for _, r0, r1 in tiles]
        parts = []
        for ci in range(R // C):
            cs = slice(ci * C, (ci + 1) * C)
            row_parts = []
            for t, (l0, r0, r1) in enumerate(tiles):
                row_parts.append(_dot_nt(q_t[cs, l0:l0 + LANES], st[t].astype(BF16)))
                kv_t = _dot_tn(v[cs, r0:r1], k_end[cs, l0:l0 + LANES])
                st[t] = st[t] * decay[ci:ci + 1, l0:l0 + LANES] + jnp.where(head_diag[t], kv_t, 0.0)
            parts.append(jnp.concatenate(row_parts, axis=1))
        for t, (_, r0, r1) in enumerate(tiles):
            state_ref[r0:r1, :] = st[t]
        o = o_intra + jnp.concatenate(parts, axis=0)

        o2 = o * o
        ms = jnp.concatenate([_pair_sums(o2[:, sl * LANES:(sl + 1) * LANES], head_a) for sl in range(vw // LANES)],
                             axis=1) * (1.0 / GLA_DV)
        y = o * lax.rsqrt(ms + EPS) * gn
        o_ref[0, rows, :] = (y * (r * jax.nn.sigmoid(r))).astype(BF16)

    prepare(0, slot0)

    def pair(m, carry):
        n0 = 2 * m
        prepare(n0 + 1, slot1)
        consume(n0, slot0)
        prepare(jnp.minimum(n0 + 2, n_blocks - 2), slot0)
        consume(n0 + 1, slot1)
        return carry

    lax.fori_loop(0, n_blocks // 2, pair, 0)


def _gla(gla, wa, ba, gn, batch, seq, n_heads, kw, vw):
    width = gla.shape[-1]
    gla3 = gla.reshape(batch, seq, width)
    return pl.pallas_call(
        functools.partial(_gla_kernel, seq=seq, n_heads=n_heads, kw=kw, vw=vw),
        name="gla",
        grid=(batch,),
        in_specs=[pl.BlockSpec((1, seq, width), lambda b: (b, 0, 0)),
                  _const_spec(wa.shape), _const_spec(ba.shape), _const_spec(gn.shape)],
        out_specs=pl.BlockSpec((1, seq, vw), lambda b: (b, 0, 0)),
        out_shape=jax.ShapeDtypeStruct((batch, seq, vw), BF16),
        scratch_shapes=[pltpu.VMEM((vw, LANES), F32)]
        + ([pltpu.VMEM((GLA_ROWS, kw), BF16)] * 3 + [pltpu.VMEM((8, kw), F32)]) * 2,
        compiler_params=pltpu.CompilerParams(dimension_semantics=("parallel",),
                                             vmem_limit_bytes=VMEM_LIMIT_BYTES),
    )(gla3, wa, ba, gn)


def _out_ffn_kernel(x_ref, oa_ref, og_ref, os_ref, wo_ref, g_ref, w1_ref, w2_ref, y_ref, *, n_a, n_g):
    mix = (_dot(oa_ref[...], wo_ref[:n_a, :]) + _dot(og_ref[...], wo_ref[n_a:n_a + n_g, :])
           + _dot(os_ref[...], wo_ref[n_a + n_g:, :]))
    x1 = x_ref[...] + mix
    ms = jnp.mean(x1 * x1, axis=-1, keepdims=True)
    h = (x1 * lax.rsqrt(ms + EPS) * g_ref[...]).astype(BF16)
    acc = x1
    d_ff = w1_ref.shape[1]
    for c0 in range(0, d_ff, FF_CHUNK):
        a = jnp.maximum(_dot(h, w1_ref[:, c0:c0 + FF_CHUNK]), 0.0)
        acc = acc + _dot((a * a).astype(BF16), w2_ref[c0:c0 + FF_CHUNK, :])
    y_ref[...] = acc


def _out_ffn(x2, oa, og, os_, wo, g2, w1, w2):
    t, d = x2.shape
    tm = ROW_TILE
    n_a, n_g, n_s = oa.shape[-1], og.shape[-1], os_.shape[-1]
    assert w1.shape[1] % FF_CHUNK == 0
    row_spec = lambda n: pl.BlockSpec((tm, n), lambda i: (i, 0))
    resident = lambda a: pl.BlockSpec(a.shape, lambda i: (0, 0), pipeline_mode=pl.Buffered(1))
    return pl.pallas_call(
        functools.partial(_out_ffn_kernel, n_a=n_a, n_g=n_g),
        name="out_proj_mlp",
        grid=(t // tm,),
        in_specs=[row_spec(d), row_spec(n_a), row_spec(n_g), row_spec(n_s),
                  resident(wo), _const_spec((1, d)), resident(w1), resident(w2)],
        out_specs=row_spec(d),
        out_shape=jax.ShapeDtypeStruct((t, d), F32),
        compiler_params=pltpu.CompilerParams(dimension_semantics=("parallel",),
                                             vmem_limit_bytes=VMEM_LIMIT_BYTES),
    )(x2, oa.reshape(t, n_a), og.reshape(t, n_g), os_.reshape(t, n_s), wo, g2, w1, w2)


def _pad_cols(w, n):
    return jnp.pad(w, ((0, 0), (0, n - w.shape[1])))


def _rope_tables(seq):
    inv = ROPE_THETA ** (-jnp.arange(0, ROPE_DIM, 2, dtype=F32) / ROPE_DIM)
    ang = jnp.arange(seq, dtype=F32)[:, None] * inv[None, :]
    pad = D_HEAD - ROPE_DIM
    cos_h = jnp.concatenate([jnp.cos(ang), jnp.cos(ang), jnp.ones((seq, pad), F32)], axis=1)
    sin_h = jnp.concatenate([-jnp.sin(ang), jnp.sin(ang), jnp.zeros((seq, pad), F32)], axis=1)
    return jnp.tile(cos_h, (1, LANES // D_HEAD)), jnp.tile(sin_h, (1, LANES // D_HEAD))


def kernel(x, norm1_g, w_in, q_norm_g, k_norm_g, gla_w_a2, gla_b_a, gla_norm_g, sg_ln_g, sg_ln_b, sg_w, sg_b,
           w_out, norm2_g, w_ff1, w_ff2):
    batch, seq, d_model = x.shape
    depth = w_in.shape[0]
    att_dim = 3 * d_model // 8
    gla_vdim = 3 * d_model // 8
    gla_heads = gla_vdim // GLA_DV
    gla_kdim = gla_heads * GLA_DK
    gla_rank = gla_w_a2.shape[1]
    sg_dim = d_model - att_dim - gla_vdim
    sg_groups = sg_dim // SG_GROUP_DIM
    kw = -(-gla_kdim // LANES) * LANES
    n_att = 3 * att_dim
    n_gla = 2 * kw + 2 * gla_vdim
    assert gla_kdim + gla_rank <= kw
    assert att_dim % LANES == 0 and gla_vdim % LANES == 0 and sg_dim % LANES == 0
    assert seq // 16 == ATT_BLOCK and seq % (2 * GLA_ROWS) == 0 and seq % PREP_ROWS == 0

    cos_t, sin_t = _rope_tables(seq)
    x2 = x.reshape(batch * seq, d_model)
    splits = [att_dim, att_dim, att_dim, gla_kdim, gla_kdim, gla_vdim, gla_vdim, gla_rank, 2 * sg_dim]
    offs = [0]
    for s in splits:
        offs.append(offs[-1] + s)

    for l in range(depth):
        w = w_in[l]
        seg = [w[:, offs[i]:offs[i + 1]] for i in range(len(splits))]
        w_packed = jnp.concatenate(
            [seg[0], seg[1], seg[2], _pad_cols(jnp.concatenate([seg[3], seg[7]], axis=1), kw),
             _pad_cols(seg[4], kw), seg[5], seg[6], seg[8]], axis=1).astype(BF16)
        w_cat = jnp.transpose(sg_w[l], (1, 0, 2)).reshape(SG_CHUNK, sg_groups * SG_CHUNK)
        sg_bias = jnp.repeat(sg_b[l].T, SG_GROUP_DIM, axis=1)
        att_heads = att_dim // D_HEAD
        qkg = jnp.concatenate([jnp.tile(q_norm_g[l], att_heads) * (D_HEAD ** -0.5 * math.log2(math.e)),
                               jnp.tile(k_norm_g[l], att_heads)])[None, :]
        att, gla, o_sg = _in_proj(x2, norm1_g[l][None, :], w_packed, sg_ln_g[l][None, :], sg_ln_b[l][None, :],
                                  w_cat, sg_bias, qkg, cos_t, sin_t, n_att, n_gla, sg_groups)

        o_att = _attention(att, batch, seq)

        wa = jnp.pad(gla_w_a2[l], ((gla_kdim, kw - gla_kdim - gla_rank), (0, kw - gla_kdim))).astype(BF16)
        ba = jnp.pad(gla_b_a[l], (0, kw - gla_kdim))[None, :]
        gn = jnp.tile(gla_norm_g[l], gla_heads)[None, :]
        o_gla = _gla(gla, wa, ba, gn, batch, seq, gla_heads, kw, gla_vdim)

        x2 = _out_ffn(x2, o_att, o_gla, o_sg, w_out[l].astype(BF16), norm2_g[l][None, :],
                      w_ff1[l].astype(BF16), w_ff2[l].astype(BF16))
    return x2.reshape(batch, seq, d_model)
```

```python
import functools
import math

import jax
import jax.numpy as jnp
from jax import lax
from jax.experimental import pallas as pl
from jax.experimental.pallas import tpu as pltpu

F32 = jnp.float32
BF16 = jnp.bfloat16

LANES = 128
D_HEAD = 64
ATT_BLOCK = 128
ROPE_DIM = D_HEAD // 4
ROPE_THETA = 500000.0
GLA_DK = 32
GLA_DV = 64
GLA_TAU = 16.0
GLA_CHUNK = 64
GLA_ROWS = 256
SG_GROUP_DIM = 64
SG_CHUNK = 128
EPS = 1e-6
MASK_VALUE = -1e30
ROW_TILE = 512
FF_CHUNK = 1024
PREP_ROWS = 512
COMBINE_ROWS = 64
VMEM_LIMIT_BYTES = 56 * 1024 * 1024


def _dot(a, b):
    return jnp.dot(a, b, preferred_element_type=F32)


def _dot_nt(a, b):
    return lax.dot_general(a, b, (((1,), (1,)), ((), ())), preferred_element_type=F32)


def _dot_tn(a, b):
    return lax.dot_general(a, b, (((0,), (0,)), ((), ())), preferred_element_type=F32)


def _split_dot_left(w, x):
    hi = x.astype(BF16)
    lo = (x - hi.astype(F32)).astype(BF16)
    return _dot(w, hi) + _dot(w, lo)


def _pair_sums(x, first_half):
    sa = jnp.sum(jnp.where(first_half, x, 0.0), axis=-1, keepdims=True)
    sb = jnp.sum(jnp.where(first_half, 0.0, x), axis=-1, keepdims=True)
    return jnp.where(first_half, sa, sb)


def _const_spec(shape):
    n = len(shape)
    return pl.BlockSpec(shape, lambda *_: (0,) * n)


def _gelu_tanh(x):
    c0 = math.sqrt(2.0 / math.pi)
    return 0.5 * x * (1.0 + jnp.tanh(c0 * (x + 0.044715 * (x * x * x))))


def _in_proj_kernel(x_ref, g_ref, w_ref, lg_ref, lb_ref, sw_ref, sb_ref, qkg_ref, cos_ref, sin_ref,
                    att_ref, gla_ref, sg_ref, *, n_att, n_gla, n_groups):
    T = SG_CHUNK
    width = sg_ref.shape[1]
    n_qk = qkg_ref.shape[1]
    half = ROPE_DIM // 2
    x = x_ref[...]
    ms = jnp.mean(x * x, axis=-1, keepdims=True)
    h = (x * lax.rsqrt(ms + EPS) * g_ref[...]).astype(BF16)
    qk_wide = _dot(h, w_ref[:, :n_qk])
    z = _dot(h, w_ref[:, n_att + n_gla:])
    att_ref[:, n_qk:] = _dot(h, w_ref[:, n_qk:n_att]).astype(BF16)
    gla_ref[...] = _dot(h, w_ref[:, n_att:n_att + n_gla]).astype(BF16)

    lane = lax.broadcasted_iota(jnp.int32, (1, LANES), 1)
    head_a = lane < D_HEAD
    first_half = lane % D_HEAD < half
    cos = cos_ref[...]
    sin_signed = sin_ref[...]
    for s0 in range(0, n_qk, LANES):
        xs = qk_wide[:, s0:s0 + LANES]
        y = xs * lax.rsqrt(_pair_sums(xs * xs, head_a) * (1.0 / D_HEAD) + EPS) * qkg_ref[:, s0:s0 + LANES]
        left = pltpu.roll(y, LANES - half, axis=1)
        right = pltpu.roll(y, half, axis=1)
        att_ref[:, s0:s0 + LANES] = (y * cos + jnp.where(first_half, left, right) * sin_signed).astype(BF16)

    row = lax.broadcasted_iota(jnp.int32, (T, n_groups * T), 0)
    col = lax.broadcasted_iota(jnp.int32, (T, n_groups * T), 1) % T
    w_cat = jnp.where(row >= col, sw_ref[...], 0.0).astype(BF16)
    lane_group = lax.broadcasted_iota(jnp.int32, (1, width), 1) // SG_GROUP_DIM
    for c in range(x.shape[0] // T):
        rows = slice(c * T, (c + 1) * T)
        u = _gelu_tanh(z[rows, :width])
        v = _gelu_tanh(z[rows, width:])
        mu = jnp.mean(v, axis=-1, keepdims=True)
        vc = v - mu
        var = jnp.mean(vc * vc, axis=-1, keepdims=True)
        vn = (vc * lax.rsqrt(var + EPS) * lg_ref[...] + lb_ref[...]).astype(BF16)
        stacked = jnp.concatenate([jnp.where(lane_group == g, vn, jnp.zeros_like(vn)) for g in range(n_groups)],
                                  axis=0)
        s = _dot(w_cat, stacked) + sb_ref[...]
        sg_ref[rows, :] = (u * s).astype(BF16)


def _in_proj(x2, g, w, lg, lb, sw, sb, qkg, cos_t, sin_t, n_att, n_gla, n_groups):
    t, d = x2.shape
    width = lg.shape[1]
    seq = cos_t.shape[0]
    tm = ROW_TILE
    assert tm % SG_CHUNK == 0 and seq % tm == 0
    pos_spec = pl.BlockSpec((tm, LANES), lambda i: (i % (seq // tm), 0))
    consts = (g, w, lg, lb, sw, sb, qkg)
    return pl.pallas_call(
        functools.partial(_in_proj_kernel, n_att=n_att, n_gla=n_gla, n_groups=n_groups),
        name="in_proj",
        grid=(t // tm,),
        in_specs=[pl.BlockSpec((tm, d), lambda i: (i, 0))] + [_const_spec(c.shape) for c in consts]
        + [pos_spec, pos_spec],
        out_specs=[pl.BlockSpec((tm, n_att), lambda i: (i, 0)),
                   pl.BlockSpec((tm, n_gla), lambda i: (i, 0)),
                   pl.BlockSpec((tm, width), lambda i: (i, 0))],
        out_shape=[jax.ShapeDtypeStruct((t, n_att), BF16),
                   jax.ShapeDtypeStruct((t, n_gla), BF16),
                   jax.ShapeDtypeStruct((t, width), BF16)],
        compiler_params=pltpu.CompilerParams(dimension_semantics=("parallel",),
                                             vmem_limit_bytes=VMEM_LIMIT_BYTES),
    )(x2, *consts, cos_t, sin_t)


def _att_kernel(q_ref, k_ref, v_ref, o_ref,
                st_q, st_k, st_v, mid_q, mid_k, mid_v,
                qa_n, qb_n, va_n, vb_n, qa_r, qb_r, k_r, va_r, vb_r, *res, seq):
    res_n, res_16, res_4, res_mid, res_nat = (res[3 * i:3 * i + 3] for i in range(5))
    L = ATT_BLOCK
    n4 = seq // 4
    n16 = seq // 16
    q4 = L // 4
    lane = lax.broadcasted_iota(jnp.int32, (1, LANES), 1)
    head_a = lane < D_HEAD
    ones_a = jnp.where(head_a, 1.0, 0.0).astype(BF16)
    ones_b = jnp.where(head_a, 0.0, 1.0).astype(BF16)

    def split_heads(x):
        return jnp.where(head_a, x, 0.0).astype(BF16), jnp.where(head_a, 0.0, x).astype(BF16)

    def prep(c, carry):
        rows = pl.ds(pl.multiple_of(c * PREP_ROWS, PREP_ROWS), PREP_ROWS)
        q = q_ref[0, rows, :]
        v = v_ref[0, rows, :]
        zero = jnp.zeros_like(q)
        qa_n[rows, :] = jnp.where(head_a, q, zero)
        qb_n[rows, :] = jnp.where(head_a, zero, q)
        va_n[rows, :LANES] = jnp.where(head_a, v, zero)
        vb_n[rows, :LANES] = jnp.where(head_a, zero, v)
        st_q[rows, :] = q.astype(F32)
        st_k[rows, :] = k_ref[0, rows, :].astype(F32)
        st_v[rows, :] = v.astype(F32)
        for ref in (va_n, va_r):
            ref[rows, LANES:] = jnp.broadcast_to(ones_a, (PREP_ROWS, LANES))
        for ref in (vb_n, vb_r):
            ref[rows, LANES:] = jnp.broadcast_to(ones_b, (PREP_ROWS, LANES))
        return carry

    lax.fori_loop(0, seq // PREP_ROWS, prep, 0)

    def regroup(st, mid, emit):
        for rho in range(4):
            mid[rho * n4:(rho + 1) * n4, :] = st[pl.ds(rho, n4, stride=4), :]
        for rho in range(4):
            for c in range(4):
                emit((rho + 4 * c) * n16, mid[pl.ds(rho * n4 + c, n16, stride=4), :])

    def emit_q(r0, blk):
        qa_r[r0:r0 + n16, :], qb_r[r0:r0 + n16, :] = split_heads(blk)

    def emit_k(r0, blk):
        k_r[r0:r0 + n16, :] = blk.astype(BF16)

    def emit_v(r0, blk):
        va_r[r0:r0 + n16, :LANES], vb_r[r0:r0 + n16, :LANES] = split_heads(blk)

    regroup(st_q, mid_q, emit_q)
    regroup(st_k, mid_k, emit_k)
    regroup(st_v, mid_v, emit_v)

    def band_bias(dist):
        return jnp.where((dist >= 0) & (dist <= L), 0.0, MASK_VALUE).astype(F32)

    qi = lax.broadcasted_iota(jnp.int32, (L, L), 0)
    kj = lax.broadcasted_iota(jnp.int32, (L, L), 1)
    qi2 = lax.broadcasted_iota(jnp.int32, (L, 2 * L), 0)
    kj2 = lax.broadcasted_iota(jnp.int32, (L, 2 * L), 1)
    bias_first = band_bias(qi - kj)
    bias_win = band_bias(qi2 + L - kj2)

    def pos4(u):
        return 4 * (u % q4) + u // q4

    bias4_first = band_bias(pos4(qi) - pos4(kj))
    bias4_win = band_bias(pos4(qi2) + L - (pos4(kj2 % L) + L * (kj2 // L)))

    def put(dst, rows, res):
        for ref, val in zip(dst, res):
            ref[rows, :] = val

    def att_core(qa, qb, kk, va, vb, bias):
        sa = _dot_nt(qa, kk) + bias
        sb = _dot_nt(qb, kk) + bias
        ma = jnp.max(sa, axis=-1, keepdims=True)
        mb = jnp.max(sb, axis=-1, keepdims=True)
        pa = jnp.exp2(sa - ma).astype(BF16)
        pb = jnp.exp2(sb - mb).astype(BF16)
        acc = _dot(pa, va) + _dot(pb, vb)
        return acc[:, :LANES], acc[:, LANES:], jnp.where(head_a, ma, mb)

    for j in range(seq // L):
        k0 = max(j - 1, 0) * L
        ks = slice(k0, (j + 1) * L)
        qs = slice(j * L, (j + 1) * L)
        put(res_n, qs, att_core(qa_n[qs, :], qb_n[qs, :], k_ref[0, ks, :], va_n[ks, :], vb_n[ks, :],
                                bias_first if j == 0 else bias_win))

    for r in range(16):
        rs = slice(r * n16, (r + 1) * n16)
        put(res_16, rs, att_core(qa_r[rs, :], qb_r[rs, :], k_r[rs, :], va_r[rs, :], vb_r[rs, :], bias_first))

    def runs(ref, rho, j):
        return [ref[(rho + 4 * c) * n16 + j * q4:(rho + 4 * c) * n16 + (j + 1) * q4, :] for c in range(4)]

    def gather4(ref, rho, j0, j1):
        return jnp.concatenate([blk for j in range(j0, j1 + 1) for blk in runs(ref, rho, j)], axis=0)

    for rho in range(4):
        for j in range(n4 // L):
            j0 = max(j - 1, 0)
            res = att_core(gather4(qa_r, rho, j, j), gather4(qb_r, rho, j, j), gather4(k_r, rho, j0, j),
                           gather4(va_r, rho, j0, j), gather4(vb_r, rho, j0, j),
                           bias4_first if j == 0 else bias4_win)
            for c in range(4):
                dst = slice((rho + 4 * c) * n16 + j * q4, (rho + 4 * c) * n16 + (j + 1) * q4)
                put(res_4, dst, [val[c * q4:(c + 1) * q4, :] for val in res])

    def combine(a, b):
        (acc_a, den_a, m_a), (acc_b, den_b, m_b) = a, b
        m = jnp.maximum(m_a, m_b)
        wa = jnp.exp2(m_a - m)
        wb = jnp.exp2(m_b - m)
        return wa * acc_a + wb * acc_b, wa * den_a + wb * den_b, m

    def get(src, rows):
        return [ref[rows, :] for ref in src]

    for c0 in range(0, seq, COMBINE_ROWS):
        rows = slice(c0, c0 + COMBINE_ROWS)
        put(res_16, rows, combine(get(res_4, rows), get(res_16, rows)))

    for src, mid, dst in zip(res_16, res_mid, res_nat):
        for rho in range(4):
            for c in range(4):
                r0 = (rho + 4 * c) * n16
                mid[pl.ds(rho * n4 + c, n16, stride=4), :] = src[r0:r0 + n16, :]
        for rho in range(4):
            dst[pl.ds(rho, n4, stride=4), :] = mid[rho * n4:(rho + 1) * n4, :]

    for c0 in range(0, seq, COMBINE_ROWS):
        rows = slice(c0, c0 + COMBINE_ROWS)
        acc, den, _ = combine(get(res_n, rows), get(res_nat, rows))
        o_ref[0, rows, :] = (acc / den).astype(BF16)


def _attention(att, batch, seq):
    n_slab = att.shape[-1] // 3 // LANES
    att3 = att.reshape(batch, seq, att.shape[-1])

    def slab_spec(off):
        return pl.BlockSpec((1, seq, LANES), lambda b, h: (b, 0, off + h))

    rows_f32 = pltpu.VMEM((seq, LANES), F32)
    rows_bf16 = pltpu.VMEM((seq, LANES), BF16)
    rows2_bf16 = pltpu.VMEM((seq, 2 * LANES), BF16)
    return pl.pallas_call(
        functools.partial(_att_kernel, seq=seq),
        name="dilated_attention",
        grid=(batch, n_slab),
        in_specs=[slab_spec(0), slab_spec(n_slab), slab_spec(2 * n_slab)],
        out_specs=pl.BlockSpec((1, seq, LANES), lambda b, h: (b, 0, h)),
        out_shape=jax.ShapeDtypeStruct((batch, seq, n_slab * LANES), BF16),
        scratch_shapes=[rows_f32] * 6
        + [rows_bf16, rows_bf16, rows2_bf16, rows2_bf16]
        + [rows_bf16, rows_bf16, rows_bf16, rows2_bf16, rows2_bf16]
        + [rows_f32] * 15,
        compiler_params=pltpu.CompilerParams(dimension_semantics=("parallel", "parallel"),
                                             vmem_limit_bytes=VMEM_LIMIT_BYTES),
    )(att3, att3, att3)


def _log_sigmoid(x):
    return jnp.minimum(x, 0.0) - jnp.log(1.0 + jnp.exp(-jnp.abs(x)))


def _gla_kernel(x_ref, wa_ref, ba_ref, gn_ref, o_ref, state_ref, *slots, seq, n_heads, kw, vw):
    C = GLA_CHUNK
    R = GLA_ROWS
    q_off, k_off, v_off, r_off = 0, kw, 2 * kw, 2 * kw + vw
    rr = lax.broadcasted_iota(jnp.int32, (R, R), 0)
    cc = lax.broadcasted_iota(jnp.int32, (R, R), 1)
    intra = (rr // C == cc // C) & (rr >= cc)
    cumsum_mat = jnp.where(intra, 1.0, 0.0).astype(BF16)
    heads_per_tile = LANES // GLA_DK
    tile_rows = heads_per_tile * GLA_DV
    tiles = [(t * LANES, t * tile_rows, min((t + 1) * tile_rows, vw)) for t in range(kw // LANES)]
    head_diag = [lax.broadcasted_iota(jnp.int32, (r1 - r0, LANES), 0) // GLA_DV
                 == lax.broadcasted_iota(jnp.int32, (r1 - r0, LANES), 1) // GLA_DK for _, r0, r1 in tiles]
    k_head = lax.broadcasted_iota(jnp.int32, (1, kw), 1) // GLA_DK
    head_a = lax.broadcasted_iota(jnp.int32, (1, LANES), 1) < GLA_DV
    wa = wa_ref[...]
    ba = ba_ref[...]
    gn = gn_ref[...]
    state_ref[...] = jnp.zeros_like(state_ref)
    n_blocks = seq // R

    def prepare(n, slot):
        rows = pl.ds(pl.multiple_of(n * R, R), R)
        qt_ref, kt_ref, ke_ref, decay_ref = slot
        a = x_ref[0, rows, q_off:q_off + kw]
        q = a.astype(F32) * (GLA_DK ** -0.5)
        k = x_ref[0, rows, k_off:k_off + kw].astype(F32)
        gk = _log_sigmoid(_dot(a, wa) + ba) * (1.0 / GLA_TAU)
        bcum = _split_dot_left(cumsum_mat, gk)
        b_last = jnp.concatenate(
            [jnp.broadcast_to(bcum[ci * C + C - 1:ci * C + C, :], (C, kw)) for ci in range(R // C)], axis=0)
        e_pos = jnp.exp(bcum)
        qt_ref[...] = (q * e_pos).astype(BF16)
        kt_ref[...] = (k * jnp.exp(-bcum)).astype(BF16)
        ke_ref[...] = (k * jnp.exp(b_last - bcum)).astype(BF16)
        for ci in range(R // C):
            decay_ref[ci:ci + 1, :] = e_pos[ci * C + C - 1:ci * C + C, :]

    def consumer(n, slot):
        rows = pl.ds(pl.multiple_of(n * R, R), R)
        ctx = {}

        def scores():
            ctx["v"] = x_ref[0, rows, v_off:v_off + vw]
            q_t, k_t = slot[0][...], slot[1][...]
            q_stack = jnp.concatenate([jnp.where(k_head == h, q_t, jnp.zeros_like(q_t)) for h in range(n_heads)],
                                      axis=0)
            ctx["q_t"] = q_t
            ctx["s_all"] = _dot_nt(q_stack, k_t)

        def intra_out():
            v, s_all = ctx["v"], ctx.pop("s_all")
            slabs = []
            for sl in range(vw // LANES):
                vs = v[:, sl * LANES:(sl + 1) * LANES]
                acc = None
                for h, v_h in ((2 * sl, jnp.where(head_a, vs, jnp.zeros_like(vs))),
                               (2 * sl + 1, jnp.where(head_a, jnp.zeros_like(vs), vs))):
                    s_h = s_all[h * R:(h + 1) * R, :].astype(BF16)
                    s_h = jnp.where(intra, s_h, jnp.zeros_like(s_h))
                    part = _dot(s_h, v_h)
                    acc = part if acc is None else acc + part
                slabs.append(acc)
            ctx["o"] = jnp.concatenate(slabs, axis=1)

        def inter_out():
            v, q_t = ctx.pop("v"), ctx.pop("q_t")
            k_end, decay = slot[2][...], slot[3][...]
            st = [state_ref[r0:r1, :] for _, r0, r1 in tiles]
            parts = []
            for ci in range(R // C):
                cs = slice(ci * C, (ci + 1) * C)
                row_parts = []
                for t, (l0, r0, r1) in enumerate(tiles):
                    row_parts.append(_dot_nt(q_t[cs, l0:l0 + LANES], st[t].astype(BF16)))
                    kv_t = _dot_tn(v[cs, r0:r1], k_end[cs, l0:l0 + LANES])
                    st[t] = st[t] * decay[ci:ci + 1, l0:l0 + LANES] + jnp.where(head_diag[t], kv_t, 0.0)
                parts.append(jnp.concatenate(row_parts, axis=1))
            for t, (_, r0, r1) in enumerate(tiles):
                state_ref[r0:r1, :] = st[t]
            ctx["o"] = ctx["o"] + jnp.concatenate(parts, axis=0)

        def finish():
            o = ctx.pop("o")
            r = x_ref[0, rows, r_off:r_off + vw].astype(F32)
            o2 = o * o
            ms = jnp.concatenate([_pair_sums(o2[:, sl * LANES:(sl + 1) * LANES], head_a)
                                  for sl in range(vw // LANES)], axis=1) * (1.0 / GLA_DV)
            y = o * lax.rsqrt(ms + EPS) * gn
            o_ref[0, rows, :] = (y * (r * jax.nn.sigmoid(r))).astype(BF16)

        return scores, intra_out, inter_out, finish

    set_x, set_y = (slots[0:4], slots[4:8]), (slots[8:12], slots[12:16])
    prepare(0, set_x[0])
    prepare(1, set_x[1])

    def half_step(n0, cur, nxt, n_prep):
        first, second = consumer(n0, cur[0]), consumer(n0 + 1, cur[1])
        prepare(n_prep, nxt[0])
        first[0]()
        prepare(n_prep + 1, nxt[1])
        second[0]()
        for ph in range(1, 4):
            first[ph]()
            second[ph]()

    def step(m, carry):
        n0 = 4 * m
        half_step(n0, set_x, set_y, n0 + 2)
        half_step(n0 + 2, set_y, set_x, jnp.minimum(n0 + 4, n_blocks - 2))
        return carry

    lax.fori_loop(0, n_blocks // 4, step, 0)


def _gla(gla, wa, ba, gn, batch, seq, n_heads, kw, vw):
    width = gla.shape[-1]
    gla3 = gla.reshape(batch, seq, width)
    return pl.pallas_call(
        functools.partial(_gla_kernel, seq=seq, n_heads=n_heads, kw=kw, vw=vw),
        name="gla",
        grid=(batch,),
        in_specs=[pl.BlockSpec((1, seq, width), lambda b: (b, 0, 0)),
                  _const_spec(wa.shape), _const_spec(ba.shape), _const_spec(gn.shape)],
        out_specs=pl.BlockSpec((1, seq, vw), lambda b: (b, 0, 0)),
        out_shape=jax.ShapeDtypeStruct((batch, seq, vw), BF16),
        scratch_shapes=[pltpu.VMEM((vw, LANES), F32)]
        + ([pltpu.VMEM((GLA_ROWS, kw), BF16)] * 3 + [pltpu.VMEM((8, kw), F32)]) * 4,
        compiler_params=pltpu.CompilerParams(dimension_semantics=("parallel",),
                                             vmem_limit_bytes=VMEM_LIMIT_BYTES),
    )(gla3, wa, ba, gn)


def _out_ffn_kernel(x_ref, oa_ref, og_ref, os_ref, wo_ref, g_ref, w1_ref, w2_ref, y_ref, *, n_a, n_g):
    mix = (_dot(oa_ref[...], wo_ref[:n_a, :]) + _dot(og_ref[...], wo_ref[n_a:n_a + n_g, :])
           + _dot(os_ref[...], wo_ref[n_a + n_g:, :]))
    x1 = x_ref[...] + mix
    ms = jnp.mean(x1 * x1, axis=-1, keepdims=True)
    h = (x1 * lax.rsqrt(ms + EPS) * g_ref[...]).astype(BF16)
    acc = x1
    d_ff = w1_ref.shape[1]
    for c0 in range(0, d_ff, FF_CHUNK):
        a = jnp.maximum(_dot(h, w1_ref[:, c0:c0 + FF_CHUNK]), 0.0)
        acc = acc + _dot((a * a).astype(BF16), w2_ref[c0:c0 + FF_CHUNK, :])
    y_ref[...] = acc


def _out_ffn(x2, oa, og, os_, wo, g2, w1, w2):
    t, d = x2.shape
    tm = ROW_TILE
    n_a, n_g, n_s = oa.shape[-1], og.shape[-1], os_.shape[-1]
    assert w1.shape[1] % FF_CHUNK == 0
    row_spec = lambda n: pl.BlockSpec((tm, n), lambda i: (i, 0))
    resident = lambda a: pl.BlockSpec(a.shape, lambda i: (0, 0), pipeline_mode=pl.Buffered(1))
    return pl.pallas_call(
        functools.partial(_out_ffn_kernel, n_a=n_a, n_g=n_g),
        name="out_proj_mlp",
        grid=(t // tm,),
        in_specs=[row_spec(d), row_spec(n_a), row_spec(n_g), row_spec(n_s),
                  resident(wo), _const_spec((1, d)), resident(w1), resident(w2)],
        out_specs=row_spec(d),
        out_shape=jax.ShapeDtypeStruct((t, d), F32),
        compiler_params=pltpu.CompilerParams(dimension_semantics=("parallel",),
                                             vmem_limit_bytes=VMEM_LIMIT_BYTES),
    )(x2, oa.reshape(t, n_a), og.reshape(t, n_g), os_.reshape(t, n_s), wo, g2, w1, w2)


def _pad_cols(w, n):
    return jnp.pad(w, ((0, 0), (0, n - w.shape[1])))


def _rope_tables(seq):
    inv = ROPE_THETA ** (-jnp.arange(0, ROPE_DIM, 2, dtype=F32) / ROPE_DIM)
    ang = jnp.arange(seq, dtype=F32)[:, None] * inv[None, :]
    pad = D_HEAD - ROPE_DIM
    cos_h = jnp.concatenate([jnp.cos(ang), jnp.cos(ang), jnp.ones((seq, pad), F32)], axis=1)
    sin_h = jnp.concatenate([-jnp.sin(ang), jnp.sin(ang), jnp.zeros((seq, pad), F32)], axis=1)
    return jnp.tile(cos_h, (1, LANES // D_HEAD)), jnp.tile(sin_h, (1, LANES // D_HEAD))


def kernel(x, norm1_g, w_in, q_norm_g, k_norm_g, gla_w_a2, gla_b_a, gla_norm_g, sg_ln_g, sg_ln_b, sg_w, sg_b,
           w_out, norm2_g, w_ff1, w_ff2):
    batch, seq, d_model = x.shape
    depth = w_in.shape[0]
    att_dim = 3 * d_model // 8
    gla_vdim = 3 * d_model // 8
    gla_heads = gla_vdim // GLA_DV
    gla_kdim = gla_heads * GLA_DK
    gla_rank = gla_w_a2.shape[1]
    sg_dim = d_model - att_dim - gla_vdim
    sg_groups = sg_dim // SG_GROUP_DIM
    kw = -(-gla_kdim // LANES) * LANES
    n_att = 3 * att_dim
    n_gla = 2 * kw + 2 * gla_vdim
    assert gla_kdim + gla_rank <= kw
    assert att_dim % LANES == 0 and gla_vdim % LANES == 0 and sg_dim % LANES == 0
    assert seq // 16 == ATT_BLOCK and seq % (4 * GLA_ROWS) == 0 and seq % PREP_ROWS == 0

    cos_t, sin_t = _rope_tables(seq)
    x2 = x.reshape(batch * seq, d_model)
    splits = [att_dim, att_dim, att_dim, gla_kdim, gla_kdim, gla_vdim, gla_vdim, gla_rank, 2 * sg_dim]
    offs = [0]
    for s in splits:
        offs.append(offs[-1] + s)

    for l in range(depth):
        w = w_in[l]
        seg = [w[:, offs[i]:offs[i + 1]] for i in range(len(splits))]
        w_packed = jnp.concatenate(
            [seg[0], seg[1], seg[2], _pad_cols(jnp.concatenate([seg[3], seg[7]], axis=1), kw),
             _pad_cols(seg[4], kw), seg[5], seg[6], seg[8]], axis=1).astype(BF16)
        w_cat = jnp.transpose(sg_w[l], (1, 0, 2)).reshape(SG_CHUNK, sg_groups * SG_CHUNK)
        sg_bias = jnp.repeat(sg_b[l].T, SG_GROUP_DIM, axis=1)
        att_heads = att_dim // D_HEAD
        qkg = jnp.concatenate([jnp.tile(q_norm_g[l], att_heads) * (D_HEAD ** -0.5 * math.log2(math.e)),
                               jnp.tile(k_norm_g[l], att_heads)])[None, :]
        att, gla, o_sg = _in_proj(x2, norm1_g[l][None, :], w_packed, sg_ln_g[l][None, :], sg_ln_b[l][None, :],
                                  w_cat, sg_bias, qkg, cos_t, sin_t, n_att, n_gla, sg_groups)

        o_att = _attention(att, batch, seq)

        wa = jnp.pad(gla_w_a2[l], ((gla_kdim, kw - gla_kdim - gla_rank), (0, kw - gla_kdim))).astype(BF16)
        ba = jnp.pad(gla_b_a[l], (0, kw - gla_kdim))[None, :]
        gn = jnp.tile(gla_norm_g[l], gla_heads)[None, :]
        o_gla = _gla(gla, wa, ba, gn, batch, seq, gla_heads, kw, gla_vdim)

        x2 = _out_ffn(x2, o_att, o_gla, o_sg, w_out[l].astype(BF16), norm2_g[l][None, :],
                      w_ff1[l].astype(BF16), w_ff2[l].astype(BF16))
    return x2.reshape(batch, seq, d_model)
```

```python
import functools
import math

import jax
import jax.numpy as jnp
from jax import lax
from jax.experimental import pallas as pl
from jax.experimental.pallas import tpu as pltpu

F32 = jnp.float32
BF16 = jnp.bfloat16

LANES = 128
D_HEAD = 64
ATT_BLOCK = 128
ROPE_DIM = D_HEAD // 4
ROPE_THETA = 500000.0
GLA_DK = 32
GLA_DV = 64
GLA_TAU = 16.0
GLA_CHUNK = 64
GLA_ROWS = 256
SG_GROUP_DIM = 64
SG_CHUNK = 128
EPS = 1e-6
MASK_VALUE = -1e30
ROW_TILE = 512
FF_CHUNK = 1024
PREP_ROWS = 512
COMBINE_ROWS = 64
VMEM_LIMIT_BYTES = 56 * 1024 * 1024


def _dot(a, b):
    return jnp.dot(a, b, preferred_element_type=F32)


def _dot_nt(a, b):
    return lax.dot_general(a, b, (((1,), (1,)), ((), ())), preferred_element_type=F32)


def _dot_tn(a, b):
    return lax.dot_general(a, b, (((0,), (0,)), ((), ())), preferred_element_type=F32)


def _split_dot_left(w, x):
    hi = x.astype(BF16)
    lo = (x - hi.astype(F32)).astype(BF16)
    return _dot(w, hi) + _dot(w, lo)


def _pair_sums(x, first_half):
    sa = jnp.sum(jnp.where(first_half, x, 0.0), axis=-1, keepdims=True)
    sb = jnp.sum(jnp.where(first_half, 0.0, x), axis=-1, keepdims=True)
    return jnp.where(first_half, sa, sb)


def _const_spec(shape):
    n = len(shape)
    return pl.BlockSpec(shape, lambda *_: (0,) * n)


def _log_sigmoid(x):
    return jnp.minimum(x, 0.0) - jnp.log(1.0 + jnp.exp(-jnp.abs(x)))


def _gla_block_phases(x, wa, ba, gn, state_ref, o_ref, row0, *, n_heads, kw, vw):
    C = GLA_CHUNK
    R = GLA_ROWS
    q_off, k_off, v_off, r_off = 0, kw, 2 * kw, 2 * kw + vw
    rr = lax.broadcasted_iota(jnp.int32, (R, R), 0)
    cc = lax.broadcasted_iota(jnp.int32, (R, R), 1)
    intra = (rr // C == cc // C) & (rr >= cc)
    cumsum_mat = jnp.where(intra, 1.0, 0.0).astype(BF16)
    heads_per_tile = LANES // GLA_DK
    tile_rows = heads_per_tile * GLA_DV
    tiles = [(t * LANES, t * tile_rows, min((t + 1) * tile_rows, vw)) for t in range(kw // LANES)]
    head_diag = [lax.broadcasted_iota(jnp.int32, (r1 - r0, LANES), 0) // GLA_DV
                 == lax.broadcasted_iota(jnp.int32, (r1 - r0, LANES), 1) // GLA_DK for _, r0, r1 in tiles]
    k_head = lax.broadcasted_iota(jnp.int32, (1, kw), 1) // GLA_DK
    head_a = lax.broadcasted_iota(jnp.int32, (1, LANES), 1) < GLA_DV
    ctx = {}

    def prepare():
        qa = x[:, q_off:q_off + kw]
        k = x[:, k_off:k_off + kw]
        gk = _log_sigmoid(_dot(qa.astype(BF16), wa) + ba) * (1.0 / GLA_TAU)
        bcum = _split_dot_left(cumsum_mat, gk)
        b_last = jnp.concatenate(
            [jnp.broadcast_to(bcum[ci * C + C - 1:ci * C + C, :], (C, kw)) for ci in range(R // C)], axis=0)
        e_pos = jnp.exp(bcum)
        ctx["q_t"] = (qa * (GLA_DK ** -0.5) * e_pos).astype(BF16)
        ctx["k_t"] = (k * jnp.exp(-bcum)).astype(BF16)
        ctx["k_end"] = (k * jnp.exp(b_last - bcum)).astype(BF16)
        ctx["decay"] = [e_pos[ci * C + C - 1:ci * C + C, :] for ci in range(R // C)]
        ctx["v"] = x[:, v_off:v_off + vw].astype(BF16)

    def scores():
        q_t = ctx["q_t"]
        q_stack = jnp.concatenate([jnp.where(k_head == h, q_t, jnp.zeros_like(q_t)) for h in range(n_heads)], axis=0)
        ctx["s_all"] = _dot_nt(q_stack, ctx.pop("k_t"))

    def intra_out():
        v, s_all = ctx["v"], ctx.pop("s_all")
        slabs = []
        for sl in range(vw // LANES):
            vs = v[:, sl * LANES:(sl + 1) * LANES]
            acc = None
            for h, v_h in ((2 * sl, jnp.where(head_a, vs, jnp.zeros_like(vs))),
                           (2 * sl + 1, jnp.where(head_a, jnp.zeros_like(vs), vs))):
                s_h = s_all[h * R:(h + 1) * R, :].astype(BF16)
                s_h = jnp.where(intra, s_h, jnp.zeros_like(s_h))
                part = _dot(s_h, v_h)
                acc = part if acc is None else acc + part
            slabs.append(acc)
        ctx["o"] = jnp.concatenate(slabs, axis=1)

    def inter_out():
        v, q_t, k_end, decay = ctx.pop("v"), ctx.pop("q_t"), ctx.pop("k_end"), ctx.pop("decay")
        st = [state_ref[r0:r1, :] for _, r0, r1 in tiles]
        parts = []
        for ci in range(R // C):
            cs = slice(ci * C, (ci + 1) * C)
            row_parts = []
            for t, (l0, r0, r1) in enumerate(tiles):
                row_parts.append(_dot_nt(q_t[cs, l0:l0 + LANES], st[t].astype(BF16)))
                kv_t = _dot_tn(v[cs, r0:r1], k_end[cs, l0:l0 + LANES])
                st[t] = st[t] * decay[ci][:, l0:l0 + LANES] + jnp.where(head_diag[t], kv_t, 0.0)
            parts.append(jnp.concatenate(row_parts, axis=1))
        for t, (_, r0, r1) in enumerate(tiles):
            state_ref[r0:r1, :] = st[t]
        ctx["o"] = ctx["o"] + jnp.concatenate(parts, axis=0)

    def finish():
        o = ctx.pop("o")
        r = x[:, r_off:r_off + vw]
        o2 = o * o
        ms = jnp.concatenate([_pair_sums(o2[:, sl * LANES:(sl + 1) * LANES], head_a)
                              for sl in range(vw // LANES)], axis=1) * (1.0 / GLA_DV)
        y = o * lax.rsqrt(ms + EPS) * gn
        o_ref[row0:row0 + R, :] = (y * (r * jax.nn.sigmoid(r))).astype(BF16)

    return prepare, scores, intra_out, inter_out, finish


def _gelu_tanh(x):
    c0 = math.sqrt(2.0 / math.pi)
    return 0.5 * x * (1.0 + jnp.tanh(c0 * (x + 0.044715 * (x * x * x))))


def _in_proj_gla_kernel(x_ref, g_ref, w_ref, lg_ref, lb_ref, sw_ref, sb_ref, qkg_ref, cos_ref, sin_ref,
                        wa_ref, ba_ref, gn_ref, att_ref, gla_ref, sg_ref, state_ref,
                        *, n_att, n_gla, n_groups, tiles_per_seq, gla_dims):
    T = SG_CHUNK
    width = sg_ref.shape[1]
    n_qk = qkg_ref.shape[1]
    half = ROPE_DIM // 2

    @pl.when(pl.program_id(0) % tiles_per_seq == 0)
    def _():
        state_ref[...] = jnp.zeros_like(state_ref)

    x = x_ref[...]
    ms = jnp.mean(x * x, axis=-1, keepdims=True)
    h = (x * lax.rsqrt(ms + EPS) * g_ref[...]).astype(BF16)
    lane = lax.broadcasted_iota(jnp.int32, (1, LANES), 1)
    head_a = lane < D_HEAD
    first_half = lane % D_HEAD < half

    def qk_epilogue(qk_wide, s0):
        xs = qk_wide[:, s0:s0 + LANES]
        y = xs * lax.rsqrt(_pair_sums(xs * xs, head_a) * (1.0 / D_HEAD) + EPS) * qkg_ref[:, s0:s0 + LANES]
        left = pltpu.roll(y, LANES - half, axis=1)
        right = pltpu.roll(y, half, axis=1)
        att_ref[:, s0:s0 + LANES] = (y * cos_ref[...] + jnp.where(first_half, left, right) * sin_ref[...]).astype(BF16)

    def sg_chunk(z, c):
        rows = slice(c * T, (c + 1) * T)
        row = lax.broadcasted_iota(jnp.int32, (T, n_groups * T), 0)
        col = lax.broadcasted_iota(jnp.int32, (T, n_groups * T), 1) % T
        w_cat = jnp.where(row >= col, sw_ref[...], 0.0).astype(BF16)
        lane_group = lax.broadcasted_iota(jnp.int32, (1, width), 1) // SG_GROUP_DIM
        u = _gelu_tanh(z[rows, :width])
        v = _gelu_tanh(z[rows, width:])
        mu = jnp.mean(v, axis=-1, keepdims=True)
        vc = v - mu
        var = jnp.mean(vc * vc, axis=-1, keepdims=True)
        vn = (vc * lax.rsqrt(var + EPS) * lg_ref[...] + lb_ref[...]).astype(BF16)
        stacked = jnp.concatenate([jnp.where(lane_group == g, vn, jnp.zeros_like(vn)) for g in range(n_groups)],
                                  axis=0)
        sg_ref[rows, :] = (u * (_dot(w_cat, stacked) + sb_ref[...])).astype(BF16)

    gla = _dot(h, w_ref[:, n_att:n_att + n_gla])
    blocks = [_gla_block_phases(gla[r0:r0 + GLA_ROWS, :], wa_ref[...], ba_ref[...], gn_ref[...], state_ref,
                                gla_ref, r0, **gla_dims) for r0 in range(0, x.shape[0], GLA_ROWS)]
    blocks[0][0]()
    qk_wide = _dot(h, w_ref[:, :n_qk])
    blocks[1][0]()
    z = _dot(h, w_ref[:, n_att + n_gla:])
    blocks[0][1]()
    blocks[1][1]()
    att_ref[:, n_qk:] = _dot(h, w_ref[:, n_qk:n_att]).astype(BF16)
    blocks[0][2]()
    for s0 in range(0, n_qk // 2, LANES):
        qk_epilogue(qk_wide, s0)
    blocks[1][2]()
    for s0 in range(n_qk // 2, n_qk, LANES):
        qk_epilogue(qk_wide, s0)
    blocks[0][3]()
    sg_chunk(z, 0)
    sg_chunk(z, 1)
    blocks[1][3]()
    sg_chunk(z, 2)
    sg_chunk(z, 3)
    blocks[0][4]()
    blocks[1][4]()


def _in_proj_gla(x2, g, w, lg, lb, sw, sb, qkg, cos_t, sin_t, wa, ba, gn, n_att, n_gla, n_groups, gla_dims):
    t, d = x2.shape
    width = lg.shape[1]
    seq = cos_t.shape[0]
    tm = ROW_TILE
    vw = gla_dims["vw"]
    assert tm == 2 * GLA_ROWS and tm == 4 * SG_CHUNK and seq % tm == 0
    pos_spec = pl.BlockSpec((tm, LANES), lambda i: (i % (seq // tm), 0))
    consts = (g, w, lg, lb, sw, sb, qkg)
    return pl.pallas_call(
        functools.partial(_in_proj_gla_kernel, n_att=n_att, n_gla=n_gla, n_groups=n_groups,
                          tiles_per_seq=seq // tm, gla_dims=gla_dims),
        name="in_proj_gla",
        grid=(t // tm,),
        in_specs=[pl.BlockSpec((tm, d), lambda i: (i, 0))] + [_const_spec(c.shape) for c in consts]
        + [pos_spec, pos_spec] + [_const_spec(c.shape) for c in (wa, ba, gn)],
        out_specs=[pl.BlockSpec((tm, n_att), lambda i: (i, 0)),
                   pl.BlockSpec((tm, vw), lambda i: (i, 0)),
                   pl.BlockSpec((tm, width), lambda i: (i, 0))],
        out_shape=[jax.ShapeDtypeStruct((t, n_att), BF16),
                   jax.ShapeDtypeStruct((t, vw), BF16),
                   jax.ShapeDtypeStruct((t, width), BF16)],
        scratch_shapes=[pltpu.VMEM((vw, LANES), F32)],
        compiler_params=pltpu.CompilerParams(dimension_semantics=("arbitrary",),
                                             vmem_limit_bytes=VMEM_LIMIT_BYTES),
    )(x2, *consts, cos_t, sin_t, wa, ba, gn)


def _att_kernel(q_ref, k_ref, v_ref, o_ref,
                st_q, st_k, st_v, mid_q, mid_k, mid_v,
                qa_n, qb_n, va_n, vb_n, qa_r, qb_r, k_r, va_r, vb_r, *res, seq):
    res_n, res_16, res_4, res_mid, res_nat = (res[3 * i:3 * i + 3] for i in range(5))
    L = ATT_BLOCK
    n4 = seq // 4
    n16 = seq // 16
    q4 = L // 4
    lane = lax.broadcasted_iota(jnp.int32, (1, LANES), 1)
    head_a = lane < D_HEAD
    ones_a = jnp.where(head_a, 1.0, 0.0).astype(BF16)
    ones_b = jnp.where(head_a, 0.0, 1.0).astype(BF16)

    def split_heads(x):
        return jnp.where(head_a, x, 0.0).astype(BF16), jnp.where(head_a, 0.0, x).astype(BF16)

    def prep(c, carry):
        rows = pl.ds(pl.multiple_of(c * PREP_ROWS, PREP_ROWS), PREP_ROWS)
        q = q_ref[0, rows, :]
        v = v_ref[0, rows, :]
        zero = jnp.zeros_like(q)
        qa_n[rows, :] = jnp.where(head_a, q, zero)
        qb_n[rows, :] = jnp.where(head_a, zero, q)
        va_n[rows, :LANES] = jnp.where(head_a, v, zero)
        vb_n[rows, :LANES] = jnp.where(head_a, zero, v)
        st_q[rows, :] = q.astype(F32)
        st_k[rows, :] = k_ref[0, rows, :].astype(F32)
        st_v[rows, :] = v.astype(F32)
        for ref in (va_n, va_r):
            ref[rows, LANES:] = jnp.broadcast_to(ones_a, (PREP_ROWS, LANES))
        for ref in (vb_n, vb_r):
            ref[rows, LANES:] = jnp.broadcast_to(ones_b, (PREP_ROWS, LANES))
        return carry

    lax.fori_loop(0, seq // PREP_ROWS, prep, 0)

    def regroup(st, mid, emit):
        for rho in range(4):
            mid[rho * n4:(rho + 1) * n4, :] = st[pl.ds(rho, n4, stride=4), :]
        for rho in range(4):
            for c in range(4):
                emit((rho + 4 * c) * n16, mid[pl.ds(rho * n4 + c, n16, stride=4), :])

    def emit_q(r0, blk):
        qa_r[r0:r0 + n16, :], qb_r[r0:r0 + n16, :] = split_heads(blk)

    def emit_k(r0, blk):
        k_r[r0:r0 + n16, :] = blk.astype(BF16)

    def emit_v(r0, blk):
        va_r[r0:r0 + n16, :LANES], vb_r[r0:r0 + n16, :LANES] = split_heads(blk)

    regroup(st_q, mid_q, emit_q)
    regroup(st_k, mid_k, emit_k)
    regroup(st_v, mid_v, emit_v)

    def band_bias(dist):
        return jnp.where((dist >= 0) & (dist <= L), 0.0, MASK_VALUE).astype(F32)

    qi = lax.broadcasted_iota(jnp.int32, (L, L), 0)
    kj = lax.broadcasted_iota(jnp.int32, (L, L), 1)
    qi2 = lax.broadcasted_iota(jnp.int32, (L, 2 * L), 0)
    kj2 = lax.broadcasted_iota(jnp.int32, (L, 2 * L), 1)
    bias_first = band_bias(qi - kj)
    bias_win = band_bias(qi2 + L - kj2)

    def pos4(u):
        return 4 * (u % q4) + u // q4

    bias4_first = band_bias(pos4(qi) - pos4(kj))
    bias4_win = band_bias(pos4(qi2) + L - (pos4(kj2 % L) + L * (kj2 // L)))

    def put(dst, rows, res):
        for ref, val in zip(dst, res):
            ref[rows, :] = val

    def att_core(qa, qb, kk, va, vb, bias):
        sa = _dot_nt(qa, kk) + bias
        sb = _dot_nt(qb, kk) + bias
        ma = jnp.max(sa, axis=-1, keepdims=True)
        mb = jnp.max(sb, axis=-1, keepdims=True)
        pa = jnp.exp2(sa - ma).astype(BF16)
        pb = jnp.exp2(sb - mb).astype(BF16)
        acc = _dot(pa, va) + _dot(pb, vb)
        return acc[:, :LANES], acc[:, LANES:], jnp.where(head_a, ma, mb)

    for j in range(seq // L):
        k0 = max(j - 1, 0) * L
        ks = slice(k0, (j + 1) * L)
        qs = slice(j * L, (j + 1) * L)
        put(res_n, qs, att_core(qa_n[qs, :], qb_n[qs, :], k_ref[0, ks, :], va_n[ks, :], vb_n[ks, :],
                                bias_first if j == 0 else bias_win))

    for r in range(16):
        rs = slice(r * n16, (r + 1) * n16)
        put(res_16, rs, att_core(qa_r[rs, :], qb_r[rs, :], k_r[rs, :], va_r[rs, :], vb_r[rs, :], bias_first))

    def runs(ref, rho, j):
        return [ref[(rho + 4 * c) * n16 + j * q4:(rho + 4 * c) * n16 + (j + 1) * q4, :] for c in range(4)]

    def gather4(ref, rho, j0, j1):
        return jnp.concatenate([blk for j in range(j0, j1 + 1) for blk in runs(ref, rho, j)], axis=0)

    for rho in range(4):
        for j in range(n4 // L):
            j0 = max(j - 1, 0)
            res = att_core(gather4(qa_r, rho, j, j), gather4(qb_r, rho, j, j), gather4(k_r, rho, j0, j),
                           gather4(va_r, rho, j0, j), gather4(vb_r, rho, j0, j),
                           bias4_first if j == 0 else bias4_win)
            for c in range(4):
                dst = slice((rho + 4 * c) * n16 + j * q4, (rho + 4 * c) * n16 + (j + 1) * q4)
                put(res_4, dst, [val[c * q4:(c + 1) * q4, :] for val in res])

    def combine(a, b):
        (acc_a, den_a, m_a), (acc_b, den_b, m_b) = a, b
        m = jnp.maximum(m_a, m_b)
        wa = jnp.exp2(m_a - m)
        wb = jnp.exp2(m_b - m)
        return wa * acc_a + wb * acc_b, wa * den_a + wb * den_b, m

    def get(src, rows):
        return [ref[rows, :] for ref in src]

    for c0 in range(0, seq, COMBINE_ROWS):
        rows = slice(c0, c0 + COMBINE_ROWS)
        put(res_16, rows, combine(get(res_4, rows), get(res_16, rows)))

    for src, mid, dst in zip(res_16, res_mid, res_nat):
        for rho in range(4):
            for c in range(4):
                r0 = (rho + 4 * c) * n16
                mid[pl.ds(rho * n4 + c, n16, stride=4), :] = src[r0:r0 + n16, :]
        for rho in range(4):
            dst[pl.ds(rho, n4, stride=4), :] = mid[rho * n4:(rho + 1) * n4, :]

    for c0 in range(0, seq, COMBINE_ROWS):
        rows = slice(c0, c0 + COMBINE_ROWS)
        acc, den, _ = combine(get(res_n, rows), get(res_nat, rows))
        o_ref[0, rows, :] = (acc / den).astype(BF16)


def _attention(att, batch, seq):
    n_slab = att.shape[-1] // 3 // LANES
    att3 = att.reshape(batch, seq, att.shape[-1])

    def slab_spec(off):
        return pl.BlockSpec((1, seq, LANES), lambda b, h: (b, 0, off + h))

    rows_f32 = pltpu.VMEM((seq, LANES), F32)
    rows_bf16 = pltpu.VMEM((seq, LANES), BF16)
    rows2_bf16 = pltpu.VMEM((seq, 2 * LANES), BF16)
    return pl.pallas_call(
        functools.partial(_att_kernel, seq=seq),
        name="dilated_attention",
        grid=(batch, n_slab),
        in_specs=[slab_spec(0), slab_spec(n_slab), slab_spec(2 * n_slab)],
        out_specs=pl.BlockSpec((1, seq, LANES), lambda b, h: (b, 0, h)),
        out_shape=jax.ShapeDtypeStruct((batch, seq, n_slab * LANES), BF16),
        scratch_shapes=[rows_f32] * 6
        + [rows_bf16, rows_bf16, rows2_bf16, rows2_bf16]
        + [rows_bf16, rows_bf16, rows_bf16, rows2_bf16, rows2_bf16]
        + [rows_f32] * 15,
        compiler_params=pltpu.CompilerParams(dimension_semantics=("parallel", "parallel"),
                                             vmem_limit_bytes=VMEM_LIMIT_BYTES),
    )(att3, att3, att3)


def _out_ffn_kernel(x_ref, oa_ref, og_ref, os_ref, wo_ref, g_ref, w1_ref, w2_ref, y_ref, *, n_a, n_g):
    mix = (_dot(oa_ref[...], wo_ref[:n_a, :]) + _dot(og_ref[...], wo_ref[n_a:n_a + n_g, :])
           + _dot(os_ref[...], wo_ref[n_a + n_g:, :]))
    x1 = x_ref[...] + mix
    ms = jnp.mean(x1 * x1, axis=-1, keepdims=True)
    h = (x1 * lax.rsqrt(ms + EPS) * g_ref[...]).astype(BF16)
    acc = x1
    d_ff = w1_ref.shape[1]
    for c0 in range(0, d_ff, FF_CHUNK):
        a = jnp.maximum(_dot(h, w1_ref[:, c0:c0 + FF_CHUNK]), 0.0)
        acc = acc + _dot((a * a).astype(BF16), w2_ref[c0:c0 + FF_CHUNK, :])
    y_ref[...] = acc


def _out_ffn(x2, oa, og, os_, wo, g2, w1, w2):
    t, d = x2.shape
    tm = ROW_TILE
    n_a, n_g, n_s = oa.shape[-1], og.shape[-1], os_.shape[-1]
    assert w1.shape[1] % FF_CHUNK == 0
    row_spec = lambda n: pl.BlockSpec((tm, n), lambda i: (i, 0))
    resident = lambda a: pl.BlockSpec(a.shape, lambda i: (0, 0), pipeline_mode=pl.Buffered(1))
    return pl.pallas_call(
        functools.partial(_out_ffn_kernel, n_a=n_a, n_g=n_g),
        name="out_proj_mlp",
        grid=(t // tm,),
        in_specs=[row_spec(d), row_spec(n_a), row_spec(n_g), row_spec(n_s),
                  resident(wo), _const_spec((1, d)), resident(w1), resident(w2)],
        out_specs=row_spec(d),
        out_shape=jax.ShapeDtypeStruct((t, d), F32),
        compiler_params=pltpu.CompilerParams(dimension_semantics=("parallel",),
                                             vmem_limit_bytes=VMEM_LIMIT_BYTES),
    )(x2, oa.reshape(t, n_a), og.reshape(t, n_g), os_.reshape(t, n_s), wo, g2, w1, w2)


def _pad_cols(w, n):
    return jnp.pad(w, ((0, 0), (0, n - w.shape[1])))


def _rope_tables(seq):
    inv = ROPE_THETA ** (-jnp.arange(0, ROPE_DIM, 2, dtype=F32) / ROPE_DIM)
    ang = jnp.arange(seq, dtype=F32)[:, None] * inv[None, :]
    pad = D_HEAD - ROPE_DIM
    cos_h = jnp.concatenate([jnp.cos(ang), jnp.cos(ang), jnp.ones((seq, pad), F32)], axis=1)
    sin_h = jnp.concatenate([-jnp.sin(ang), jnp.sin(ang), jnp.zeros((seq, pad), F32)], axis=1)
    return jnp.tile(cos_h, (1, LANES // D_HEAD)), jnp.tile(sin_h, (1, LANES // D_HEAD))


def kernel(x, norm1_g, w_in, q_norm_g, k_norm_g, gla_w_a2, gla_b_a, gla_norm_g, sg_ln_g, sg_ln_b, sg_w, sg_b,
           w_out, norm2_g, w_ff1, w_ff2):
    batch, seq, d_model = x.shape
    depth = w_in.shape[0]
    att_dim = 3 * d_model // 8
    gla_vdim = 3 * d_model // 8
    gla_heads = gla_vdim // GLA_DV
    gla_kdim = gla_heads * GLA_DK
    gla_rank = gla_w_a2.shape[1]
    sg_dim = d_model - att_dim - gla_vdim
    sg_groups = sg_dim // SG_GROUP_DIM
    kw = -(-gla_kdim // LANES) * LANES
    n_att = 3 * att_dim
    n_gla = 2 * kw + 2 * gla_vdim
    assert gla_kdim + gla_rank <= kw
    assert att_dim % LANES == 0 and gla_vdim % LANES == 0 and sg_dim % LANES == 0
    assert seq // 16 == ATT_BLOCK and seq % PREP_ROWS == 0

    cos_t, sin_t = _rope_tables(seq)
    x2 = x.reshape(batch * seq, d_model)
    splits = [att_dim, att_dim, att_dim, gla_kdim, gla_kdim, gla_vdim, gla_vdim, gla_rank, 2 * sg_dim]
    offs = [0]
    for s in splits:
        offs.append(offs[-1] + s)

    for l in range(depth):
        w = w_in[l]
        seg = [w[:, offs[i]:offs[i + 1]] for i in range(len(splits))]
        w_packed = jnp.concatenate(
            [seg[0], seg[1], seg[2], _pad_cols(jnp.concatenate([seg[3], seg[7]], axis=1), kw),
             _pad_cols(seg[4], kw), seg[5], seg[6], seg[8]], axis=1).astype(BF16)
        w_cat = jnp.transpose(sg_w[l], (1, 0, 2)).reshape(SG_CHUNK, sg_groups * SG_CHUNK)
        sg_bias = jnp.repeat(sg_b[l].T, SG_GROUP_DIM, axis=1)
        att_heads = att_dim // D_HEAD
        qkg = jnp.concatenate([jnp.tile(q_norm_g[l], att_heads) * (D_HEAD ** -0.5 * math.log2(math.e)),
                               jnp.tile(k_norm_g[l], att_heads)])[None, :]
        wa = jnp.pad(gla_w_a2[l], ((gla_kdim, kw - gla_kdim - gla_rank), (0, kw - gla_kdim))).astype(BF16)
        ba = jnp.pad(gla_b_a[l], (0, kw - gla_kdim))[None, :]
        gn = jnp.tile(gla_norm_g[l], gla_heads)[None, :]
        att, o_gla, o_sg = _in_proj_gla(x2, norm1_g[l][None, :], w_packed, sg_ln_g[l][None, :],
                                        sg_ln_b[l][None, :], w_cat, sg_bias, qkg, cos_t, sin_t, wa, ba, gn,
                                        n_att, n_gla, sg_groups, dict(n_heads=gla_heads, kw=kw, vw=gla_vdim))

        o_att = _attention(att, batch, seq)

        x2 = _out_ffn(x2, o_att, o_gla, o_sg, w_out[l].astype(BF16), norm2_g[l][None, :],
                      w_ff1[l].astype(BF16), w_ff2[l].astype(BF16))
    return x2.reshape(batch, seq, d_model)
```

```python
import functools
import math

import jax
import jax.numpy as jnp
from jax import lax
from jax.experimental import pallas as pl
from jax.experimental.pallas import tpu as pltpu

F32 = jnp.float32
BF16 = jnp.bfloat16

LANES = 128
D_HEAD = 64
ATT_BLOCK = 128
ROPE_DIM = D_HEAD // 4
ROPE_THETA = 500000.0
GLA_DK = 32
GLA_DV = 64
GLA_TAU = 16.0
GLA_CHUNK = 64
GLA_ROWS = 256
SG_GROUP_DIM = 64
SG_CHUNK = 128
EPS = 1e-6
MASK_VALUE = -1e30
ROW_TILE = 512
FF_CHUNK = 1024
PREP_ROWS = 512
COMBINE_ROWS = 64
VMEM_LIMIT_BYTES = 56 * 1024 * 1024


def _dot(a, b):
    return jnp.dot(a, b, preferred_element_type=F32)


def _dot_nt(a, b):
    return lax.dot_general(a, b, (((1,), (1,)), ((), ())), preferred_element_type=F32)


def _dot_tn(a, b):
    return lax.dot_general(a, b, (((0,), (0,)), ((), ())), preferred_element_type=F32)


def _split_dot_left(w, x):
    hi = x.astype(BF16)
    lo = (x - hi.astype(F32)).astype(BF16)
    return _dot(w, hi) + _dot(w, lo)


def _pair_sums(x, first_half):
    sa = jnp.sum(jnp.where(first_half, x, 0.0), axis=-1, keepdims=True)
    sb = jnp.sum(jnp.where(first_half, 0.0, x), axis=-1, keepdims=True)
    return jnp.where(first_half, sa, sb)


def _const_spec(shape):
    n = len(shape)
    return pl.BlockSpec(shape, lambda *_: (0,) * n)


def _log_sigmoid(x):
    return jnp.minimum(x, 0.0) - jnp.log(1.0 + jnp.exp(-jnp.abs(x)))


def _gla_block_phases(x, wa, ba, gn, state_ref, o_ref, row0, *, n_heads, kw, vw):
    C = GLA_CHUNK
    R = GLA_ROWS
    q_off, k_off, v_off, r_off = 0, kw, 2 * kw, 2 * kw + vw
    rr = lax.broadcasted_iota(jnp.int32, (R, R), 0)
    cc = lax.broadcasted_iota(jnp.int32, (R, R), 1)
    intra = (rr // C == cc // C) & (rr >= cc)
    cumsum_mat = jnp.where(intra, 1.0, 0.0).astype(BF16)
    heads_per_tile = LANES // GLA_DK
    tile_rows = heads_per_tile * GLA_DV
    tiles = [(t * LANES, t * tile_rows, min((t + 1) * tile_rows, vw)) for t in range(kw // LANES)]
    head_diag = [lax.broadcasted_iota(jnp.int32, (r1 - r0, LANES), 0) // GLA_DV
                 == lax.broadcasted_iota(jnp.int32, (r1 - r0, LANES), 1) // GLA_DK for _, r0, r1 in tiles]
    k_head = lax.broadcasted_iota(jnp.int32, (1, kw), 1) // GLA_DK
    head_a = lax.broadcasted_iota(jnp.int32, (1, LANES), 1) < GLA_DV
    ctx = {}

    def prepare():
        qa = x[:, q_off:q_off + kw]
        k = x[:, k_off:k_off + kw]
        gk = _log_sigmoid(_dot(qa.astype(BF16), wa) + ba) * (1.0 / GLA_TAU)
        bcum = _split_dot_left(cumsum_mat, gk)
        b_last = jnp.concatenate(
            [jnp.broadcast_to(bcum[ci * C + C - 1:ci * C + C, :], (C, kw)) for ci in range(R // C)], axis=0)
        e_pos = jnp.exp(bcum)
        ctx["q_t"] = (qa * (GLA_DK ** -0.5) * e_pos).astype(BF16)
        ctx["k_t"] = (k * jnp.exp(-bcum)).astype(BF16)
        ctx["k_end"] = (k * jnp.exp(b_last - bcum)).astype(BF16)
        ctx["decay"] = [e_pos[ci * C + C - 1:ci * C + C, :] for ci in range(R // C)]
        ctx["v"] = x[:, v_off:v_off + vw].astype(BF16)

    def scores():
        q_t = ctx["q_t"]
        q_stack = jnp.concatenate([jnp.where(k_head == h, q_t, jnp.zeros_like(q_t)) for h in range(n_heads)], axis=0)
        ctx["s_all"] = _dot_nt(q_stack, ctx.pop("k_t"))

    def intra_out():
        v, s_all = ctx["v"], ctx.pop("s_all")
        slabs = []
        for sl in range(vw // LANES):
            vs = v[:, sl * LANES:(sl + 1) * LANES]
            acc = None
            for h, v_h in ((2 * sl, jnp.where(head_a, vs, jnp.zeros_like(vs))),
                           (2 * sl + 1, jnp.where(head_a, jnp.zeros_like(vs), vs))):
                s_h = s_all[h * R:(h + 1) * R, :].astype(BF16)
                s_h = jnp.where(intra, s_h, jnp.zeros_like(s_h))
                part = _dot(s_h, v_h)
                acc = part if acc is None else acc + part
            slabs.append(acc)
        ctx["o"] = jnp.concatenate(slabs, axis=1)

    def inter_out():
        v, q_t, k_end, decay = ctx.pop("v"), ctx.pop("q_t"), ctx.pop("k_end"), ctx.pop("decay")
        st = [state_ref[r0:r1, :] for _, r0, r1 in tiles]
        parts = []
        for ci in range(R // C):
            cs = slice(ci * C, (ci + 1) * C)
            row_parts = []
            for t, (l0, r0, r1) in enumerate(tiles):
                row_parts.append(_dot_nt(q_t[cs, l0:l0 + LANES], st[t].astype(BF16)))
                kv_t = _dot_tn(v[cs, r0:r1], k_end[cs, l0:l0 + LANES])
                st[t] = st[t] * decay[ci][:, l0:l0 + LANES] + jnp.where(head_diag[t], kv_t, 0.0)
            parts.append(jnp.concatenate(row_parts, axis=1))
        for t, (_, r0, r1) in enumerate(tiles):
            state_ref[r0:r1, :] = st[t]
        ctx["o"] = ctx["o"] + jnp.concatenate(parts, axis=0)

    def finish():
        o = ctx.pop("o")
        r = x[:, r_off:r_off + vw]
        o2 = o * o
        ms = jnp.concatenate([_pair_sums(o2[:, sl * LANES:(sl + 1) * LANES], head_a)
                              for sl in range(vw // LANES)], axis=1) * (1.0 / GLA_DV)
        y = o * lax.rsqrt(ms + EPS) * gn
        o_ref[row0:row0 + R, :] = (y * (r * jax.nn.sigmoid(r))).astype(BF16)

    return prepare, scores, intra_out, inter_out, finish


def _gelu_tanh(x):
    c0 = math.sqrt(2.0 / math.pi)
    return 0.5 * x * (1.0 + jnp.tanh(c0 * (x + 0.044715 * (x * x * x))))


def _in_proj_gla_kernel(x_ref, g_ref, w_ref, lg_ref, lb_ref, sw_ref, sb_ref, qkg_ref, cos_ref, sin_ref,
                        wa_ref, ba_ref, gn_ref, att_ref, gla_ref, sg_ref, state_ref,
                        *, n_att, n_gla, n_groups, tiles_per_seq, gla_dims):
    T = SG_CHUNK
    width = sg_ref.shape[1]
    n_qk = qkg_ref.shape[1]
    half = ROPE_DIM // 2

    @pl.when(pl.program_id(0) % tiles_per_seq == 0)
    def _():
        state_ref[...] = jnp.zeros_like(state_ref)

    x = x_ref[...]
    ms = jnp.mean(x * x, axis=-1, keepdims=True)
    h = (x * lax.rsqrt(ms + EPS) * g_ref[...]).astype(BF16)
    lane = lax.broadcasted_iota(jnp.int32, (1, LANES), 1)
    head_a = lane < D_HEAD
    first_half = lane % D_HEAD < half

    def qk_epilogue(qk_wide, s0):
        xs = qk_wide[:, s0:s0 + LANES]
        y = xs * lax.rsqrt(_pair_sums(xs * xs, head_a) * (1.0 / D_HEAD) + EPS) * qkg_ref[:, s0:s0 + LANES]
        left = pltpu.roll(y, LANES - half, axis=1)
        right = pltpu.roll(y, half, axis=1)
        att_ref[:, s0:s0 + LANES] = (y * cos_ref[...] + jnp.where(first_half, left, right) * sin_ref[...]).astype(BF16)

    def sg_chunk(z, c):
        rows = slice(c * T, (c + 1) * T)
        row = lax.broadcasted_iota(jnp.int32, (T, n_groups * T), 0)
        col = lax.broadcasted_iota(jnp.int32, (T, n_groups * T), 1) % T
        w_cat = jnp.where(row >= col, sw_ref[...], 0.0).astype(BF16)
        lane_group = lax.broadcasted_iota(jnp.int32, (1, width), 1) // SG_GROUP_DIM
        u = _gelu_tanh(z[rows, :width])
        v = _gelu_tanh(z[rows, width:])
        mu = jnp.mean(v, axis=-1, keepdims=True)
        vc = v - mu
        var = jnp.mean(vc * vc, axis=-1, keepdims=True)
        vn = (vc * lax.rsqrt(var + EPS) * lg_ref[...] + lb_ref[...]).astype(BF16)
        stacked = jnp.concatenate([jnp.where(lane_group == g, vn, jnp.zeros_like(vn)) for g in range(n_groups)],
                                  axis=0)
        sg_ref[rows, :] = (u * (_dot(w_cat, stacked) + sb_ref[...])).astype(BF16)

    gla = _dot(h, w_ref[:, n_att:n_att + n_gla])
    blocks = [_gla_block_phases(gla[r0:r0 + GLA_ROWS, :], wa_ref[...], ba_ref[...], gn_ref[...], state_ref,
                                gla_ref, r0, **gla_dims) for r0 in range(0, x.shape[0], GLA_ROWS)]
    blocks[0][0]()
    qk_wide = _dot(h, w_ref[:, :n_qk])
    blocks[1][0]()
    z = _dot(h, w_ref[:, n_att + n_gla:])
    blocks[0][1]()
    blocks[1][1]()
    att_ref[:, n_qk:] = _dot(h, w_ref[:, n_qk:n_att]).astype(BF16)
    blocks[0][2]()
    for s0 in range(0, n_qk // 2, LANES):
        qk_epilogue(qk_wide, s0)
    blocks[1][2]()
    for s0 in range(n_qk // 2, n_qk, LANES):
        qk_epilogue(qk_wide, s0)
    blocks[0][3]()
    sg_chunk(z, 0)
    sg_chunk(z, 1)
    blocks[1][3]()
    sg_chunk(z, 2)
    sg_chunk(z, 3)
    blocks[0][4]()
    blocks[1][4]()


def _in_proj_gla(x2, g, w, lg, lb, sw, sb, qkg, cos_t, sin_t, wa, ba, gn, n_att, n_gla, n_groups, gla_dims):
    t, d = x2.shape
    width = lg.shape[1]
    seq = cos_t.shape[0]
    tm = ROW_TILE
    vw = gla_dims["vw"]
    assert tm == 2 * GLA_ROWS and tm == 4 * SG_CHUNK and seq % tm == 0
    pos_spec = pl.BlockSpec((tm, LANES), lambda i: (i % (seq // tm), 0))
    consts = (g, w, lg, lb, sw, sb, qkg)
    return pl.pallas_call(
        functools.partial(_in_proj_gla_kernel, n_att=n_att, n_gla=n_gla, n_groups=n_groups,
                          tiles_per_seq=seq // tm, gla_dims=gla_dims),
        name="in_proj_gla",
        grid=(t // tm,),
        in_specs=[pl.BlockSpec((tm, d), lambda i: (i, 0))] + [_const_spec(c.shape) for c in consts]
        + [pos_spec, pos_spec] + [_const_spec(c.shape) for c in (wa, ba, gn)],
        out_specs=[pl.BlockSpec((tm, n_att), lambda i: (i, 0)),
                   pl.BlockSpec((tm, vw), lambda i: (i, 0)),
                   pl.BlockSpec((tm, width), lambda i: (i, 0))],
        out_shape=[jax.ShapeDtypeStruct((t, n_att), BF16),
                   jax.ShapeDtypeStruct((t, vw), BF16),
                   jax.ShapeDtypeStruct((t, width), BF16)],
        scratch_shapes=[pltpu.VMEM((vw, LANES), F32)],
        compiler_params=pltpu.CompilerParams(dimension_semantics=("arbitrary",),
                                             vmem_limit_bytes=VMEM_LIMIT_BYTES),
    )(x2, *consts, cos_t, sin_t, wa, ba, gn)


def _att_kernel(q_ref, k_ref, v_ref, o_ref,
                st_q, st_k, st_v, mid_q, mid_k, mid_v,
                qa_n, qb_n, va_n, vb_n, qa_r, qb_r, k_r, va_r, vb_r, *res, seq):
    res_n, res_16, res_4, res_mid, res_nat = (res[3 * i:3 * i + 3] for i in range(5))
    L = ATT_BLOCK
    n4 = seq // 4
    n16 = seq // 16
    q4 = L // 4
    lane = lax.broadcasted_iota(jnp.int32, (1, LANES), 1)
    head_a = lane < D_HEAD
    ones_a = jnp.where(head_a, 1.0, 0.0).astype(BF16)
    ones_b = jnp.where(head_a, 0.0, 1.0).astype(BF16)

    def split_heads(x):
        return jnp.where(head_a, x, 0.0).astype(BF16), jnp.where(head_a, 0.0, x).astype(BF16)

    def prep(c, carry):
        rows = slice(c * PREP_ROWS, (c + 1) * PREP_ROWS)
        q = q_ref[0, rows, :]
        v = v_ref[0, rows, :]
        zero = jnp.zeros_like(q)
        qa_n[rows, :] = jnp.where(head_a, q, zero)
        qb_n[rows, :] = jnp.where(head_a, zero, q)
        va_n[rows, :LANES] = jnp.where(head_a, v, zero)
        vb_n[rows, :LANES] = jnp.where(head_a, zero, v)
        st_q[rows, :] = q.astype(F32)
        st_k[rows, :] = k_ref[0, rows, :].astype(F32)
        st_v[rows, :] = v.astype(F32)
        for ref in (va_n, va_r):
            ref[rows, LANES:] = jnp.broadcast_to(ones_a, (PREP_ROWS, LANES))
        for ref in (vb_n, vb_r):
            ref[rows, LANES:] = jnp.broadcast_to(ones_b, (PREP_ROWS, LANES))
        return carry

    for c in range(seq // PREP_ROWS):
        prep(c, 0)

    def regroup(st, mid, emit):
        for rho in range(4):
            mid[rho * n4:(rho + 1) * n4, :] = st[pl.ds(rho, n4, stride=4), :]
        for rho in range(4):
            for c in range(4):
                emit((rho + 4 * c) * n16, mid[pl.ds(rho * n4 + c, n16, stride=4), :])

    def emit_q(r0, blk):
        qa_r[r0:r0 + n16, :], qb_r[r0:r0 + n16, :] = split_heads(blk)

    def emit_k(r0, blk):
        k_r[r0:r0 + n16, :] = blk.astype(BF16)

    def emit_v(r0, blk):
        va_r[r0:r0 + n16, :LANES], vb_r[r0:r0 + n16, :LANES] = split_heads(blk)

    regroup(st_q, mid_q, emit_q)
    regroup(st_k, mid_k, emit_k)
    regroup(st_v, mid_v, emit_v)

    def band_bias(dist):
        return jnp.where((dist >= 0) & (dist <= L), 0.0, MASK_VALUE).astype(F32)

    qi = lax.broadcasted_iota(jnp.int32, (L, L), 0)
    kj = lax.broadcasted_iota(jnp.int32, (L, L), 1)
    qi2 = lax.broadcasted_iota(jnp.int32, (L, 2 * L), 0)
    kj2 = lax.broadcasted_iota(jnp.int32, (L, 2 * L), 1)
    bias_first = band_bias(qi - kj)
    bias_win = band_bias(qi2 + L - kj2)

    def pos4(u):
        return 4 * (u % q4) + u // q4

    bias4_first = band_bias(pos4(qi) - pos4(kj))
    bias4_win = band_bias(pos4(qi2) + L - (pos4(kj2 % L) + L * (kj2 // L)))

    def put(dst, rows, res):
        for ref, val in zip(dst, res):
            ref[rows, :] = val

    def att_core(qa, qb, kk, va, vb, bias):
        sa = _dot_nt(qa, kk) + bias
        sb = _dot_nt(qb, kk) + bias
        ma = jnp.max(sa, axis=-1, keepdims=True)
        mb = jnp.max(sb, axis=-1, keepdims=True)
        pa = jnp.exp2(sa - ma).astype(BF16)
        pb = jnp.exp2(sb - mb).astype(BF16)
        acc = _dot(pa, va) + _dot(pb, vb)
        return acc[:, :LANES], acc[:, LANES:], jnp.where(head_a, ma, mb)

    for j in range(seq // L):
        k0 = max(j - 1, 0) * L
        ks = slice(k0, (j + 1) * L)
        qs = slice(j * L, (j + 1) * L)
        put(res_n, qs, att_core(qa_n[qs, :], qb_n[qs, :], k_ref[0, ks, :], va_n[ks, :], vb_n[ks, :],
                                bias_first if j == 0 else bias_win))

    for r in range(16):
        rs = slice(r * n16, (r + 1) * n16)
        put(res_16, rs, att_core(qa_r[rs, :], qb_r[rs, :], k_r[rs, :], va_r[rs, :], vb_r[rs, :], bias_first))

    def runs(ref, rho, j):
        return [ref[(rho + 4 * c) * n16 + j * q4:(rho + 4 * c) * n16 + (j + 1) * q4, :] for c in range(4)]

    def gather4(ref, rho, j0, j1):
        return jnp.concatenate([blk for j in range(j0, j1 + 1) for blk in runs(ref, rho, j)], axis=0)

    for rho in range(4):
        for j in range(n4 // L):
            j0 = max(j - 1, 0)
            res = att_core(gather4(qa_r, rho, j, j), gather4(qb_r, rho, j, j), gather4(k_r, rho, j0, j),
                           gather4(va_r, rho, j0, j), gather4(vb_r, rho, j0, j),
                           bias4_first if j == 0 else bias4_win)
            for c in range(4):
                dst = slice((rho + 4 * c) * n16 + j * q4, (rho + 4 * c) * n16 + (j + 1) * q4)
                put(res_4, dst, [val[c * q4:(c + 1) * q4, :] for val in res])

    def combine(a, b):
        (acc_a, den_a, m_a), (acc_b, den_b, m_b) = a, b
        m = jnp.maximum(m_a, m_b)
        wa = jnp.exp2(m_a - m)
        wb = jnp.exp2(m_b - m)
        return wa * acc_a + wb * acc_b, wa * den_a + wb * den_b, m

    def get(src, rows):
        return [ref[rows, :] for ref in src]

    for c0 in range(0, seq, COMBINE_ROWS):
        rows = slice(c0, c0 + COMBINE_ROWS)
        put(res_16, rows, combine(get(res_4, rows), get(res_16, rows)))

    for src, mid, dst in zip(res_16, res_mid, res_nat):
        for rho in range(4):
            for c in range(4):
                r0 = (rho + 4 * c) * n16
                mid[pl.ds(rho * n4 + c, n16, stride=4), :] = src[r0:r0 + n16, :]
        for rho in range(4):
            dst[pl.ds(rho, n4, stride=4), :] = mid[rho * n4:(rho + 1) * n4, :]

    for c0 in range(0, seq, COMBINE_ROWS):
        rows = slice(c0, c0 + COMBINE_ROWS)
        acc, den, _ = combine(get(res_n, rows), get(res_nat, rows))
        o_ref[0, rows, :] = (acc / den).astype(BF16)


def _attention(att, batch, seq):
    n_slab = att.shape[-1] // 3 // LANES
    att3 = att.reshape(batch, seq, att.shape[-1])

    def slab_spec(off):
        return pl.BlockSpec((1, seq, LANES), lambda b, h: (b, 0, off + h))

    rows_f32 = pltpu.VMEM((seq, LANES), F32)
    rows_bf16 = pltpu.VMEM((seq, LANES), BF16)
    rows2_bf16 = pltpu.VMEM((seq, 2 * LANES), BF16)
    return pl.pallas_call(
        functools.partial(_att_kernel, seq=seq),
        name="dilated_attention",
        grid=(batch, n_slab),
        in_specs=[slab_spec(0), slab_spec(n_slab), slab_spec(2 * n_slab)],
        out_specs=pl.BlockSpec((1, seq, LANES), lambda b, h: (b, 0, h)),
        out_shape=jax.ShapeDtypeStruct((batch, seq, n_slab * LANES), BF16),
        scratch_shapes=[rows_f32] * 6
        + [rows_bf16, rows_bf16, rows2_bf16, rows2_bf16]
        + [rows_bf16, rows_bf16, rows_bf16, rows2_bf16, rows2_bf16]
        + [rows_f32] * 15,
        compiler_params=pltpu.CompilerParams(dimension_semantics=("parallel", "parallel"),
                                             vmem_limit_bytes=VMEM_LIMIT_BYTES),
    )(att3, att3, att3)


def _out_ffn_kernel(x_ref, oa_ref, og_ref, os_ref, wo_ref, g_ref, w1_ref, w2_ref, y_ref, *, n_a, n_g):
    mix = (_dot(oa_ref[...], wo_ref[:n_a, :]) + _dot(og_ref[...], wo_ref[n_a:n_a + n_g, :])
           + _dot(os_ref[...], wo_ref[n_a + n_g:, :]))
    x1 = x_ref[...] + mix
    ms = jnp.mean(x1 * x1, axis=-1, keepdims=True)
    h = (x1 * lax.rsqrt(ms + EPS) * g_ref[...]).astype(BF16)
    acc = x1
    d_ff = w1_ref.shape[1]
    for c0 in range(0, d_ff, FF_CHUNK):
        a = jnp.maximum(_dot(h, w1_ref[:, c0:c0 + FF_CHUNK]), 0.0)
        acc = acc + _dot((a * a).astype(BF16), w2_ref[c0:c0 + FF_CHUNK, :])
    y_ref[...] = acc


def _out_ffn(x2, oa, og, os_, wo, g2, w1, w2):
    t, d = x2.shape
    tm = ROW_TILE
    n_a, n_g, n_s = oa.shape[-1], og.shape[-1], os_.shape[-1]
    assert w1.shape[1] % FF_CHUNK == 0
    row_spec = lambda n: pl.BlockSpec((tm, n), lambda i: (i, 0))
    resident = lambda a: pl.BlockSpec(a.shape, lambda i: (0, 0), pipeline_mode=pl.Buffered(1))
    return pl.pallas_call(
        functools.partial(_out_ffn_kernel, n_a=n_a, n_g=n_g),
        name="out_proj_mlp",
        grid=(t // tm,),
        in_specs=[row_spec(d), row_spec(n_a), row_spec(n_g), row_spec(n_s),
                  resident(wo), _const_spec((1, d)), resident(w1), resident(w2)],
        out_specs=row_spec(d),
        out_shape=jax.ShapeDtypeStruct((t, d), F32),
        compiler_params=pltpu.CompilerParams(dimension_semantics=("parallel",),
                                             vmem_limit_bytes=VMEM_LIMIT_BYTES),
    )(x2, oa.reshape(t, n_a), og.reshape(t, n_g), os_.reshape(t, n_s), wo, g2, w1, w2)


def _pad_cols(w, n):
    return jnp.pad(w, ((0, 0), (0, n - w.shape[1])))


def _rope_tables(seq):
    inv = ROPE_THETA ** (-jnp.arange(0, ROPE_DIM, 2, dtype=F32) / ROPE_DIM)
    ang = jnp.arange(seq, dtype=F32)[:, None] * inv[None, :]
    pad = D_HEAD - ROPE_DIM
    cos_h = jnp.concatenate([jnp.cos(ang), jnp.cos(ang), jnp.ones((seq, pad), F32)], axis=1)
    sin_h = jnp.concatenate([-jnp.sin(ang), jnp.sin(ang), jnp.zeros((seq, pad), F32)], axis=1)
    return jnp.tile(cos_h, (1, LANES // D_HEAD)), jnp.tile(sin_h, (1, LANES // D_HEAD))


def kernel(x, norm1_g, w_in, q_norm_g, k_norm_g, gla_w_a2, gla_b_a, gla_norm_g, sg_ln_g, sg_ln_b, sg_w, sg_b,
           w_out, norm2_g, w_ff1, w_ff2):
    batch, seq, d_model = x.shape
    depth = w_in.shape[0]
    att_dim = 3 * d_model // 8
    gla_vdim = 3 * d_model // 8
    gla_heads = gla_vdim // GLA_DV
    gla_kdim = gla_heads * GLA_DK
    gla_rank = gla_w_a2.shape[1]
    sg_dim = d_model - att_dim - gla_vdim
    sg_groups = sg_dim // SG_GROUP_DIM
    kw = -(-gla_kdim // LANES) * LANES
    n_att = 3 * att_dim
    n_gla = 2 * kw + 2 * gla_vdim
    assert gla_kdim + gla_rank <= kw
    assert att_dim % LANES == 0 and gla_vdim % LANES == 0 and sg_dim % LANES == 0
    assert seq // 16 == ATT_BLOCK and seq % PREP_ROWS == 0

    cos_t, sin_t = _rope_tables(seq)
    x2 = x.reshape(batch * seq, d_model)
    splits = [att_dim, att_dim, att_dim, gla_kdim, gla_kdim, gla_vdim, gla_vdim, gla_rank, 2 * sg_dim]
    offs = [0]
    for s in splits:
        offs.append(offs[-1] + s)

    for l in range(depth):
        w = w_in[l]
        seg = [w[:, offs[i]:offs[i + 1]] for i in range(len(splits))]
        w_packed = jnp.concatenate(
            [seg[0], seg[1], seg[2], _pad_cols(jnp.concatenate([seg[3], seg[7]], axis=1), kw),
             _pad_cols(seg[4], kw), seg[5], seg[6], seg[8]], axis=1).astype(BF16)
        w_cat = jnp.transpose(sg_w[l], (1, 0, 2)).reshape(SG_CHUNK, sg_groups * SG_CHUNK)
        sg_bias = jnp.repeat(sg_b[l].T, SG_GROUP_DIM, axis=1)
        att_heads = att_dim // D_HEAD
        qkg = jnp.concatenate([jnp.tile(q_norm_g[l], att_heads) * (D_HEAD ** -0.5 * math.log2(math.e)),
                               jnp.tile(k_norm_g[l], att_heads)])[None, :]
        wa = jnp.pad(gla_w_a2[l], ((gla_kdim, kw - gla_kdim - gla_rank), (0, kw - gla_kdim))).astype(BF16)
        ba = jnp.pad(gla_b_a[l], (0, kw - gla_kdim))[None, :]
        gn = jnp.tile(gla_norm_g[l], gla_heads)[None, :]
        att, o_gla, o_sg = _in_proj_gla(x2, norm1_g[l][None, :], w_packed, sg_ln_g[l][None, :],
                                        sg_ln_b[l][None, :], w_cat, sg_bias, qkg, cos_t, sin_t, wa, ba, gn,
                                        n_att, n_gla, sg_groups, dict(n_heads=gla_heads, kw=kw, vw=gla_vdim))

        o_att = _attention(att, batch, seq)

        x2 = _out_ffn(x2, o_att, o_gla, o_sg, w_out[l].astype(BF16), norm2_g[l][None, :],
                      w_ff1[l].astype(BF16), w_ff2[l].astype(BF16))
    return x2.reshape(batch, seq, d_model)
```

```python
import functools
import math

import jax
import jax.numpy as jnp
from jax import lax
from jax.experimental import pallas as pl
from jax.experimental.pallas import tpu as pltpu

F32 = jnp.float32
BF16 = jnp.bfloat16

LANES = 128
D_HEAD = 64
ATT_BLOCK = 128
ROPE_DIM = D_HEAD // 4
ROPE_THETA = 500000.0
GLA_DK = 32
GLA_DV = 64
GLA_TAU = 16.0
GLA_CHUNK = 64
GLA_ROWS = 256
SG_GROUP_DIM = 64
SG_CHUNK = 128
EPS = 1e-6
MASK_VALUE = -1e30
ROW_TILE = 512
FF_CHUNK = 1024
PREP_ROWS = 512
COMBINE_ROWS = 64
VMEM_LIMIT_BYTES = 56 * 1024 * 1024


def _dot(a, b):
    return jnp.dot(a, b, preferred_element_type=F32)


def _dot_nt(a, b):
    return lax.dot_general(a, b, (((1,), (1,)), ((), ())), preferred_element_type=F32)


def _dot_tn(a, b):
    return lax.dot_general(a, b, (((0,), (0,)), ((), ())), preferred_element_type=F32)


def _split_dot_left(w, x):
    hi = x.astype(BF16)
    lo = (x - hi.astype(F32)).astype(BF16)
    return _dot(w, hi) + _dot(w, lo)


def _pair_sums(x, first_half):
    sa = jnp.sum(jnp.where(first_half, x, 0.0), axis=-1, keepdims=True)
    sb = jnp.sum(jnp.where(first_half, 0.0, x), axis=-1, keepdims=True)
    return jnp.where(first_half, sa, sb)


def _const_spec(shape):
    n = len(shape)
    return pl.BlockSpec(shape, lambda *_: (0,) * n)


def _log_sigmoid(x):
    return jnp.minimum(x, 0.0) - jnp.log(1.0 + jnp.exp(-jnp.abs(x)))


def _gla_block_phases(x, wa, ba, gn, state_ref, o_ref, row0, *, n_heads, kw, vw):
    C = GLA_CHUNK
    R = GLA_ROWS
    q_off, k_off, v_off, r_off = 0, kw, 2 * kw, 2 * kw + vw
    rr = lax.broadcasted_iota(jnp.int32, (R, R), 0)
    cc = lax.broadcasted_iota(jnp.int32, (R, R), 1)
    intra = (rr // C == cc // C) & (rr >= cc)
    cumsum_mat = jnp.where(intra, 1.0, 0.0).astype(BF16)
    heads_per_tile = LANES // GLA_DK
    tile_rows = heads_per_tile * GLA_DV
    tiles = [(t * LANES, t * tile_rows, min((t + 1) * tile_rows, vw)) for t in range(kw // LANES)]
    head_diag = [lax.broadcasted_iota(jnp.int32, (r1 - r0, LANES), 0) // GLA_DV
                 == lax.broadcasted_iota(jnp.int32, (r1 - r0, LANES), 1) // GLA_DK for _, r0, r1 in tiles]
    k_head = lax.broadcasted_iota(jnp.int32, (1, kw), 1) // GLA_DK
    head_a = lax.broadcasted_iota(jnp.int32, (1, LANES), 1) < GLA_DV
    ctx = {}

    def prepare():
        qa = x[:, q_off:q_off + kw]
        k = x[:, k_off:k_off + kw]
        gk = _log_sigmoid(_dot(qa.astype(BF16), wa) + ba) * (1.0 / GLA_TAU)
        bcum = _split_dot_left(cumsum_mat, gk)
        b_last = jnp.concatenate(
            [jnp.broadcast_to(bcum[ci * C + C - 1:ci * C + C, :], (C, kw)) for ci in range(R // C)], axis=0)
        e_pos = jnp.exp(bcum)
        ctx["q_t"] = (qa * (GLA_DK ** -0.5) * e_pos).astype(BF16)
        ctx["k_t"] = (k * jnp.exp(-bcum)).astype(BF16)
        ctx["k_end"] = (k * jnp.exp(b_last - bcum)).astype(BF16)
        ctx["decay"] = [e_pos[ci * C + C - 1:ci * C + C, :] for ci in range(R // C)]
        ctx["v"] = x[:, v_off:v_off + vw].astype(BF16)

    def scores():
        q_t = ctx["q_t"]
        q_stack = jnp.concatenate([jnp.where(k_head == h, q_t, jnp.zeros_like(q_t)) for h in range(n_heads)], axis=0)
        ctx["s_all"] = _dot_nt(q_stack, ctx.pop("k_t"))

    def intra_out():
        v, s_all = ctx["v"], ctx.pop("s_all")
        slabs = []
        for sl in range(vw // LANES):
            vs = v[:, sl * LANES:(sl + 1) * LANES]
            acc = None
            for h, v_h in ((2 * sl, jnp.where(head_a, vs, jnp.zeros_like(vs))),
                           (2 * sl + 1, jnp.where(head_a, jnp.zeros_like(vs), vs))):
                s_h = s_all[h * R:(h + 1) * R, :].astype(BF16)
                s_h = jnp.where(intra, s_h, jnp.zeros_like(s_h))
                part = _dot(s_h, v_h)
                acc = part if acc is None else acc + part
            slabs.append(acc)
        ctx["o"] = jnp.concatenate(slabs, axis=1)

    def inter_out():
        v, q_t, k_end, decay = ctx.pop("v"), ctx.pop("q_t"), ctx.pop("k_end"), ctx.pop("decay")
        st = [state_ref[r0:r1, :] for _, r0, r1 in tiles]
        parts = []
        for ci in range(R // C):
            cs = slice(ci * C, (ci + 1) * C)
            row_parts = []
            for t, (l0, r0, r1) in enumerate(tiles):
                row_parts.append(_dot_nt(q_t[cs, l0:l0 + LANES], st[t].astype(BF16)))
                kv_t = _dot_tn(v[cs, r0:r1], k_end[cs, l0:l0 + LANES])
                st[t] = st[t] * decay[ci][:, l0:l0 + LANES] + jnp.where(head_diag[t], kv_t, 0.0)
            parts.append(jnp.concatenate(row_parts, axis=1))
        for t, (_, r0, r1) in enumerate(tiles):
            state_ref[r0:r1, :] = st[t]
        ctx["o"] = ctx["o"] + jnp.concatenate(parts, axis=0)

    def finish():
        o = ctx.pop("o")
        r = x[:, r_off:r_off + vw]
        o2 = o * o
        ms = jnp.concatenate([_pair_sums(o2[:, sl * LANES:(sl + 1) * LANES], head_a)
                              for sl in range(vw // LANES)], axis=1) * (1.0 / GLA_DV)
        y = o * lax.rsqrt(ms + EPS) * gn
        o_ref[row0:row0 + R, :] = (y * (r * jax.nn.sigmoid(r))).astype(BF16)

    return prepare, scores, intra_out, inter_out, finish


def _gelu_tanh(x):
    c0 = math.sqrt(2.0 / math.pi)
    return 0.5 * x * (1.0 + jnp.tanh(c0 * (x + 0.044715 * (x * x * x))))


def _in_proj_gla_kernel(x_ref, g_ref, w_ref, lg_ref, lb_ref, sw_ref, sb_ref, qkg_ref, cos_ref, sin_ref,
                        wa_ref, ba_ref, gn_ref, att_ref, gla_ref, sg_ref, state_ref,
                        *, n_att, n_gla, n_groups, tiles_per_seq, gla_dims):
    T = SG_CHUNK
    width = sg_ref.shape[1]
    n_qk = qkg_ref.shape[1]
    half = ROPE_DIM // 2

    @pl.when(pl.program_id(0) % tiles_per_seq == 0)
    def _():
        state_ref[...] = jnp.zeros_like(state_ref)

    x = x_ref[...]
    ms = jnp.mean(x * x, axis=-1, keepdims=True)
    h = (x * lax.rsqrt(ms + EPS) * g_ref[...]).astype(BF16)
    lane = lax.broadcasted_iota(jnp.int32, (1, LANES), 1)
    head_a = lane < D_HEAD
    first_half = lane % D_HEAD < half

    def qk_epilogue(qk_wide, s0):
        xs = qk_wide[:, s0:s0 + LANES]
        y = xs * lax.rsqrt(_pair_sums(xs * xs, head_a) * (1.0 / D_HEAD) + EPS) * qkg_ref[:, s0:s0 + LANES]
        left = pltpu.roll(y, LANES - half, axis=1)
        right = pltpu.roll(y, half, axis=1)
        att_ref[:, s0:s0 + LANES] = (y * cos_ref[...] + jnp.where(first_half, left, right) * sin_ref[...]).astype(BF16)

    def sg_chunk(z, c):
        rows = slice(c * T, (c + 1) * T)
        row = lax.broadcasted_iota(jnp.int32, (T, n_groups * T), 0)
        col = lax.broadcasted_iota(jnp.int32, (T, n_groups * T), 1) % T
        w_cat = jnp.where(row >= col, sw_ref[...], 0.0).astype(BF16)
        lane_group = lax.broadcasted_iota(jnp.int32, (1, width), 1) // SG_GROUP_DIM
        u = _gelu_tanh(z[rows, :width])
        v = _gelu_tanh(z[rows, width:])
        mu = jnp.mean(v, axis=-1, keepdims=True)
        vc = v - mu
        var = jnp.mean(vc * vc, axis=-1, keepdims=True)
        vn = (vc * lax.rsqrt(var + EPS) * lg_ref[...] + lb_ref[...]).astype(BF16)
        stacked = jnp.concatenate([jnp.where(lane_group == g, vn, jnp.zeros_like(vn)) for g in range(n_groups)],
                                  axis=0)
        sg_ref[rows, :] = (u * (_dot(w_cat, stacked) + sb_ref[...])).astype(BF16)

    gla = _dot(h, w_ref[:, n_att:n_att + n_gla])
    blocks = [_gla_block_phases(gla[r0:r0 + GLA_ROWS, :], wa_ref[...], ba_ref[...], gn_ref[...], state_ref,
                                gla_ref, r0, **gla_dims) for r0 in range(0, x.shape[0], GLA_ROWS)]
    blocks[0][0]()
    qk_wide = _dot(h, w_ref[:, :n_qk])
    blocks[1][0]()
    z = _dot(h, w_ref[:, n_att + n_gla:])
    blocks[0][1]()
    blocks[1][1]()
    att_ref[:, n_qk:] = _dot(h, w_ref[:, n_qk:n_att]).astype(BF16)
    blocks[0][2]()
    for s0 in range(0, n_qk // 2, LANES):
        qk_epilogue(qk_wide, s0)
    blocks[1][2]()
    for s0 in range(n_qk // 2, n_qk, LANES):
        qk_epilogue(qk_wide, s0)
    blocks[0][3]()
    sg_chunk(z, 0)
    sg_chunk(z, 1)
    blocks[1][3]()
    sg_chunk(z, 2)
    sg_chunk(z, 3)
    blocks[0][4]()
    blocks[1][4]()


def _in_proj_gla(x2, g, w, lg, lb, sw, sb, qkg, cos_t, sin_t, wa, ba, gn, n_att, n_gla, n_groups, gla_dims):
    t, d = x2.shape
    width = lg.shape[1]
    seq = cos_t.shape[0]
    tm = ROW_TILE
    vw = gla_dims["vw"]
    assert tm == 2 * GLA_ROWS and tm == 4 * SG_CHUNK and seq % tm == 0
    pos_spec = pl.BlockSpec((tm, LANES), lambda i: (i % (seq // tm), 0))
    consts = (g, w, lg, lb, sw, sb, qkg)
    return pl.pallas_call(
        functools.partial(_in_proj_gla_kernel, n_att=n_att, n_gla=n_gla, n_groups=n_groups,
                          tiles_per_seq=seq // tm, gla_dims=gla_dims),
        name="in_proj_gla",
        grid=(t // tm,),
        in_specs=[pl.BlockSpec((tm, d), lambda i: (i, 0))] + [_const_spec(c.shape) for c in consts]
        + [pos_spec, pos_spec] + [_const_spec(c.shape) for c in (wa, ba, gn)],
        out_specs=[pl.BlockSpec((tm, n_att), lambda i: (i, 0)),
                   pl.BlockSpec((tm, vw), lambda i: (i, 0)),
                   pl.BlockSpec((tm, width), lambda i: (i, 0))],
        out_shape=[jax.ShapeDtypeStruct((t, n_att), BF16),
                   jax.ShapeDtypeStruct((t, vw), BF16),
                   jax.ShapeDtypeStruct((t, width), BF16)],
        scratch_shapes=[pltpu.VMEM((vw, LANES), F32)],
        compiler_params=pltpu.CompilerParams(dimension_semantics=("arbitrary",),
                                             vmem_limit_bytes=VMEM_LIMIT_BYTES),
    )(x2, *consts, cos_t, sin_t, wa, ba, gn)


def _att_kernel(q_ref, k_ref, v_ref, o_ref,
                st_q, st_k, st_v, mid_q, mid_k, mid_v,
                qa_n, qb_n, va_n, vb_n, qa_r, qb_r, k_r, va_r, vb_r, *res, seq):
    res_n, res_16, res_4, res_mid, res_nat = (res[3 * i:3 * i + 3] for i in range(5))
    L = ATT_BLOCK
    n4 = seq // 4
    n16 = seq // 16
    q4 = L // 4
    lane = lax.broadcasted_iota(jnp.int32, (1, LANES), 1)
    head_a = lane < D_HEAD
    ones_a = jnp.where(head_a, 1.0, 0.0).astype(BF16)
    ones_b = jnp.where(head_a, 0.0, 1.0).astype(BF16)

    def split_heads(x):
        return jnp.where(head_a, x, 0.0).astype(BF16), jnp.where(head_a, 0.0, x).astype(BF16)

    for c0 in range(0, seq, PREP_ROWS):
        rows = slice(c0, c0 + PREP_ROWS)
        q = q_ref[0, rows, :]
        v = v_ref[0, rows, :]
        zero = jnp.zeros_like(q)
        qa_n[rows, :] = jnp.where(head_a, q, zero)
        qb_n[rows, :] = jnp.where(head_a, zero, q)
        va_n[rows, :LANES] = jnp.where(head_a, v, zero)
        vb_n[rows, :LANES] = jnp.where(head_a, zero, v)
        st_q[rows, :] = q.astype(F32)
        st_k[rows, :] = k_ref[0, rows, :].astype(F32)
        st_v[rows, :] = v.astype(F32)
        for ref in (va_n, va_r):
            ref[rows, LANES:] = jnp.broadcast_to(ones_a, (PREP_ROWS, LANES))
        for ref in (vb_n, vb_r):
            ref[rows, LANES:] = jnp.broadcast_to(ones_b, (PREP_ROWS, LANES))

    def regroup(st, mid, emit):
        for rho in range(4):
            mid[rho * n4:(rho + 1) * n4, :] = st[pl.ds(rho, n4, stride=4), :]
        for rho in range(4):
            for c in range(4):
                emit((rho + 4 * c) * n16, mid[pl.ds(rho * n4 + c, n16, stride=4), :])

    def emit_q(r0, blk):
        qa_r[r0:r0 + n16, :], qb_r[r0:r0 + n16, :] = split_heads(blk)

    def emit_k(r0, blk):
        k_r[r0:r0 + n16, :] = blk.astype(BF16)

    def emit_v(r0, blk):
        va_r[r0:r0 + n16, :LANES], vb_r[r0:r0 + n16, :LANES] = split_heads(blk)

    regroup(st_q, mid_q, emit_q)
    regroup(st_k, mid_k, emit_k)
    regroup(st_v, mid_v, emit_v)

    def band_bias(dist):
        return jnp.where((dist >= 0) & (dist <= L), 0.0, MASK_VALUE).astype(F32)

    qi = lax.broadcasted_iota(jnp.int32, (L, L), 0)
    kj = lax.broadcasted_iota(jnp.int32, (L, L), 1)
    qi2 = lax.broadcasted_iota(jnp.int32, (L, 2 * L), 0)
    kj2 = lax.broadcasted_iota(jnp.int32, (L, 2 * L), 1)
    bias_first = band_bias(qi - kj)
    bias_win = band_bias(qi2 + L - kj2)

    def pos4(u):
        return 4 * (u % q4) + u // q4

    bias4_first = band_bias(pos4(qi) - pos4(kj))
    bias4_win = band_bias(pos4(qi2) + L - (pos4(kj2 % L) + L * (kj2 // L)))

    def put(dst, rows, res):
        for ref, val in zip(dst, res):
            ref[rows, :] = val

    def att_core(qa, qb, kk, va, vb, bias):
        sa = _dot_nt(qa, kk) + bias
        sb = _dot_nt(qb, kk) + bias
        ma = jnp.max(sa, axis=-1, keepdims=True)
        mb = jnp.max(sb, axis=-1, keepdims=True)
        pa = jnp.exp2(sa - ma).astype(BF16)
        pb = jnp.exp2(sb - mb).astype(BF16)
        acc = _dot(pa, va) + _dot(pb, vb)
        return acc[:, :LANES], acc[:, LANES:], jnp.where(head_a, ma, mb)

    for j in range(seq // L):
        k0 = max(j - 1, 0) * L
        ks = slice(k0, (j + 1) * L)
        qs = slice(j * L, (j + 1) * L)
        put(res_n, qs, att_core(qa_n[qs, :], qb_n[qs, :], k_ref[0, ks, :], va_n[ks, :], vb_n[ks, :],
                                bias_first if j == 0 else bias_win))

    for r in range(16):
        rs = slice(r * n16, (r + 1) * n16)
        put(res_16, rs, att_core(qa_r[rs, :], qb_r[rs, :], k_r[rs, :], va_r[rs, :], vb_r[rs, :], bias_first))

    def runs(ref, rho, j):
        return [ref[(rho + 4 * c) * n16 + j * q4:(rho + 4 * c) * n16 + (j + 1) * q4, :] for c in range(4)]

    def gather4(ref, rho, j0, j1):
        return jnp.concatenate([blk for j in range(j0, j1 + 1) for blk in runs(ref, rho, j)], axis=0)

    for rho in range(4):
        for j in range(n4 // L):
            j0 = max(j - 1, 0)
            res = att_core(gather4(qa_r, rho, j, j), gather4(qb_r, rho, j, j), gather4(k_r, rho, j0, j),
                           gather4(va_r, rho, j0, j), gather4(vb_r, rho, j0, j),
                           bias4_first if j == 0 else bias4_win)
            for c in range(4):
                dst = slice((rho + 4 * c) * n16 + j * q4, (rho + 4 * c) * n16 + (j + 1) * q4)
                put(res_4, dst, [val[c * q4:(c + 1) * q4, :] for val in res])

    def combine(a, b):
        (acc_a, den_a, m_a), (acc_b, den_b, m_b) = a, b
        m = jnp.maximum(m_a, m_b)
        wa = jnp.exp2(m_a - m)
        wb = jnp.exp2(m_b - m)
        return wa * acc_a + wb * acc_b, wa * den_a + wb * den_b, m

    def get(src, rows):
        return [ref[rows, :] for ref in src]

    for c0 in range(0, seq, COMBINE_ROWS):
        rows = slice(c0, c0 + COMBINE_ROWS)
        put(res_16, rows, combine(get(res_4, rows), get(res_16, rows)))

    for src, mid, dst in zip(res_16, res_mid, res_nat):
        for rho in range(4):
            for c in range(4):
                r0 = (rho + 4 * c) * n16
                mid[pl.ds(rho * n4 + c, n16, stride=4), :] = src[r0:r0 + n16, :]
        for rho in range(4):
            dst[pl.ds(rho, n4, stride=4), :] = mid[rho * n4:(rho + 1) * n4, :]

    for c0 in range(0, seq, COMBINE_ROWS):
        rows = slice(c0, c0 + COMBINE_ROWS)
        acc, den, _ = combine(get(res_n, rows), get(res_nat, rows))
        o_ref[0, rows, :] = (acc / den).astype(BF16)


def _attention(att, batch, seq):
    n_slab = att.shape[-1] // 3 // LANES
    att3 = att.reshape(batch, seq, att.shape[-1])

    def slab_spec(off):
        return pl.BlockSpec((1, seq, LANES), lambda b, h: (b, 0, off + h))

    rows_f32 = pltpu.VMEM((seq, LANES), F32)
    rows_bf16 = pltpu.VMEM((seq, LANES), BF16)
    rows2_bf16 = pltpu.VMEM((seq, 2 * LANES), BF16)
    return pl.pallas_call(
        functools.partial(_att_kernel, seq=seq),
        name="dilated_attention",
        grid=(batch, n_slab),
        in_specs=[slab_spec(0), slab_spec(n_slab), slab_spec(2 * n_slab)],
        out_specs=pl.BlockSpec((1, seq, LANES), lambda b, h: (b, 0, h)),
        out_shape=jax.ShapeDtypeStruct((batch, seq, n_slab * LANES), BF16),
        scratch_shapes=[rows_f32] * 6
        + [rows_bf16, rows_bf16, rows2_bf16, rows2_bf16]
        + [rows_bf16, rows_bf16, rows_bf16, rows2_bf16, rows2_bf16]
        + [rows_f32] * 15,
        compiler_params=pltpu.CompilerParams(dimension_semantics=("parallel", "parallel"),
                                             vmem_limit_bytes=VMEM_LIMIT_BYTES),
    )(att3, att3, att3)


def _out_ffn_kernel(x_ref, oa_ref, og_ref, os_ref, wo_ref, g_ref, w1_ref, w2_ref, y_ref, *, n_a, n_g):
    mix = (_dot(oa_ref[...], wo_ref[:n_a, :]) + _dot(og_ref[...], wo_ref[n_a:n_a + n_g, :])
           + _dot(os_ref[...], wo_ref[n_a + n_g:, :]))
    x1 = x_ref[...] + mix
    ms = jnp.mean(x1 * x1, axis=-1, keepdims=True)
    h = (x1 * lax.rsqrt(ms + EPS) * g_ref[...]).astype(BF16)
    acc = x1
    d_ff = w1_ref.shape[1]
    for c0 in range(0, d_ff, FF_CHUNK):
        a = jnp.maximum(_dot(h, w1_ref[:, c0:c0 + FF_CHUNK]), 0.0)
        acc = acc + _dot((a * a).astype(BF16), w2_ref[c0:c0 + FF_CHUNK, :])
    y_ref[...] = acc


def _out_ffn(x2, oa, og, os_, wo, g2, w1, w2):
    t, d = x2.shape
    tm = ROW_TILE
    n_a, n_g, n_s = oa.shape[-1], og.shape[-1], os_.shape[-1]
    assert w1.shape[1] % FF_CHUNK == 0
    row_spec = lambda n: pl.BlockSpec((tm, n), lambda i: (i, 0))
    resident = lambda a: pl.BlockSpec(a.shape, lambda i: (0, 0), pipeline_mode=pl.Buffered(1))
    return pl.pallas_call(
        functools.partial(_out_ffn_kernel, n_a=n_a, n_g=n_g),
        name="out_proj_mlp",
        grid=(t // tm,),
        in_specs=[row_spec(d), row_spec(n_a), row_spec(n_g), row_spec(n_s),
                  resident(wo), _const_spec((1, d)), resident(w1), resident(w2)],
        out_specs=row_spec(d),
        out_shape=jax.ShapeDtypeStruct((t, d), F32),
        compiler_params=pltpu.CompilerParams(dimension_semantics=("parallel",),
                                             vmem_limit_bytes=VMEM_LIMIT_BYTES),
    )(x2, oa.reshape(t, n_a), og.reshape(t, n_g), os_.reshape(t, n_s), wo, g2, w1, w2)


def _pad_cols(w, n):
    return jnp.pad(w, ((0, 0), (0, n - w.shape[1])))


def _rope_tables(seq):
    inv = ROPE_THETA ** (-jnp.arange(0, ROPE_DIM, 2, dtype=F32) / ROPE_DIM)
    ang = jnp.arange(seq, dtype=F32)[:, None] * inv[None, :]
    pad = D_HEAD - ROPE_DIM
    cos_h = jnp.concatenate([jnp.cos(ang), jnp.cos(ang), jnp.ones((seq, pad), F32)], axis=1)
    sin_h = jnp.concatenate([-jnp.sin(ang), jnp.sin(ang), jnp.zeros((seq, pad), F32)], axis=1)
    return jnp.tile(cos_h, (1, LANES // D_HEAD)), jnp.tile(sin_h, (1, LANES // D_HEAD))


def kernel(x, norm1_g, w_in, q_norm_g, k_norm_g, gla_w_a2, gla_b_a, gla_norm_g, sg_ln_g, sg_ln_b, sg_w, sg_b,
           w_out, norm2_g, w_ff1, w_ff2):
    batch, seq, d_model = x.shape
    depth = w_in.shape[0]
    att_dim = 3 * d_model // 8
    gla_vdim = 3 * d_model // 8
    gla_heads = gla_vdim // GLA_DV
    gla_kdim = gla_heads * GLA_DK
    gla_rank = gla_w_a2.shape[1]
    sg_dim = d_model - att_dim - gla_vdim
    sg_groups = sg_dim // SG_GROUP_DIM
    kw = -(-gla_kdim // LANES) * LANES
    n_att = 3 * att_dim
    n_gla = 2 * kw + 2 * gla_vdim
    assert gla_kdim + gla_rank <= kw
    assert att_dim % LANES == 0 and gla_vdim % LANES == 0 and sg_dim % LANES == 0
    assert seq // 16 == ATT_BLOCK and seq % PREP_ROWS == 0

    cos_t, sin_t = _rope_tables(seq)
    x2 = x.reshape(batch * seq, d_model)
    splits = [att_dim, att_dim, att_dim, gla_kdim, gla_kdim, gla_vdim, gla_vdim, gla_rank, 2 * sg_dim]
    offs = [0]
    for s in splits:
        offs.append(offs[-1] + s)

    for l in range(depth):
        w = w_in[l]
        seg = [w[:, offs[i]:offs[i + 1]] for i in range(len(splits))]
        w_packed = jnp.concatenate(
            [seg[0], seg[1], seg[2], _pad_cols(jnp.concatenate([seg[3], seg[7]], axis=1), kw),
             _pad_cols(seg[4], kw), seg[5], seg[6], seg[8]], axis=1).astype(BF16)
        w_cat = jnp.transpose(sg_w[l], (1, 0, 2)).reshape(SG_CHUNK, sg_groups * SG_CHUNK)
        sg_bias = jnp.repeat(sg_b[l].T, SG_GROUP_DIM, axis=1)
        att_heads = att_dim // D_HEAD
        qkg = jnp.concatenate([jnp.tile(q_norm_g[l], att_heads) * (D_HEAD ** -0.5 * math.log2(math.e)),
                               jnp.tile(k_norm_g[l], att_heads)])[None, :]
        wa = jnp.pad(gla_w_a2[l], ((gla_kdim, kw - gla_kdim - gla_rank), (0, kw - gla_kdim))).astype(BF16)
        ba = jnp.pad(gla_b_a[l], (0, kw - gla_kdim))[None, :]
        gn = jnp.tile(gla_norm_g[l], gla_heads)[None, :]
        att, o_gla, o_sg = _in_proj_gla(x2, norm1_g[l][None, :], w_packed, sg_ln_g[l][None, :],
                                        sg_ln_b[l][None, :], w_cat, sg_bias, qkg, cos_t, sin_t, wa, ba, gn,
                                        n_att, n_gla, sg_groups, dict(n_heads=gla_heads, kw=kw, vw=gla_vdim))

        o_att = _attention(att, batch, seq)

        x2 = _out_ffn(x2, o_att, o_gla, o_sg, w_out[l].astype(BF16), norm2_g[l][None, :],
                      w_ff1[l].astype(BF16), w_ff2[l].astype(BF16))
    return x2.reshape(batch, seq, d_model)
```

```python
import functools
import math

import jax
import jax.numpy as jnp
from jax import lax
from jax.experimental import pallas as pl
from jax.experimental.pallas import tpu as pltpu

F32 = jnp.float32
BF16 = jnp.bfloat16

LANES = 128
D_HEAD = 64
ATT_BLOCK = 128
ROPE_DIM = D_HEAD // 4
ROPE_THETA = 500000.0
GLA_DK = 32
GLA_DV = 64
GLA_TAU = 16.0
GLA_CHUNK = 64
GLA_ROWS = 256
SG_GROUP_DIM = 64
SG_CHUNK = 128
EPS = 1e-6
MASK_VALUE = -1e30
ROW_TILE = 512
FF_CHUNK = 1024
PREP_ROWS = 512
COMBINE_ROWS = 64
VMEM_LIMIT_BYTES = 56 * 1024 * 1024


def _dot(a, b):
    return jnp.dot(a, b, preferred_element_type=F32)


def _dot_nt(a, b):
    return lax.dot_general(a, b, (((1,), (1,)), ((), ())), preferred_element_type=F32)


def _dot_tn(a, b):
    return lax.dot_general(a, b, (((0,), (0,)), ((), ())), preferred_element_type=F32)


def _split_dot_left(w, x):
    hi = x.astype(BF16)
    lo = (x - hi.astype(F32)).astype(BF16)
    return _dot(w, hi) + _dot(w, lo)


def _pair_sums(x, first_half):
    sa = jnp.sum(jnp.where(first_half, x, 0.0), axis=-1, keepdims=True)
    sb = jnp.sum(jnp.where(first_half, 0.0, x), axis=-1, keepdims=True)
    return jnp.where(first_half, sa, sb)


def _const_spec(shape):
    n = len(shape)
    return pl.BlockSpec(shape, lambda *_: (0,) * n)


def _log_sigmoid(x):
    return jnp.minimum(x, 0.0) - jnp.log(1.0 + jnp.exp(-jnp.abs(x)))


def _gla_block_phases(x, wa, ba, gn, state_ref, o_ref, row0, *, n_heads, kw, vw):
    C = GLA_CHUNK
    R = GLA_ROWS
    q_off, k_off, v_off, r_off = 0, kw, 2 * kw, 2 * kw + vw
    rr = lax.broadcasted_iota(jnp.int32, (R, R), 0)
    cc = lax.broadcasted_iota(jnp.int32, (R, R), 1)
    intra = (rr // C == cc // C) & (rr >= cc)
    cumsum_mat = jnp.where(intra, 1.0, 0.0).astype(BF16)
    heads_per_tile = LANES // GLA_DK
    tile_rows = heads_per_tile * GLA_DV
    tiles = [(t * LANES, t * tile_rows, min((t + 1) * tile_rows, vw)) for t in range(kw // LANES)]
    head_diag = [lax.broadcasted_iota(jnp.int32, (r1 - r0, LANES), 0) // GLA_DV
                 == lax.broadcasted_iota(jnp.int32, (r1 - r0, LANES), 1) // GLA_DK for _, r0, r1 in tiles]
    k_head = lax.broadcasted_iota(jnp.int32, (1, kw), 1) // GLA_DK
    head_a = lax.broadcasted_iota(jnp.int32, (1, LANES), 1) < GLA_DV
    ctx = {}

    def prepare():
        qa = x[:, q_off:q_off + kw]
        k = x[:, k_off:k_off + kw]
        gk = _log_sigmoid(_dot(qa.astype(BF16), wa) + ba) * (1.0 / GLA_TAU)
        bcum = _split_dot_left(cumsum_mat, gk)
        b_last = jnp.concatenate(
            [jnp.broadcast_to(bcum[ci * C + C - 1:ci * C + C, :], (C, kw)) for ci in range(R // C)], axis=0)
        e_pos = jnp.exp(bcum)
        ctx["q_t"] = (qa * (GLA_DK ** -0.5) * e_pos).astype(BF16)
        ctx["k_t"] = (k * jnp.exp(-bcum)).astype(BF16)
        ctx["k_end"] = (k * jnp.exp(b_last - bcum)).astype(BF16)
        ctx["decay"] = [e_pos[ci * C + C - 1:ci * C + C, :] for ci in range(R // C)]
        ctx["v"] = x[:, v_off:v_off + vw].astype(BF16)

    def scores():
        q_t = ctx["q_t"]
        q_stack = jnp.concatenate([jnp.where(k_head == h, q_t, jnp.zeros_like(q_t)) for h in range(n_heads)], axis=0)
        ctx["s_all"] = _dot_nt(q_stack, ctx.pop("k_t"))

    def intra_out():
        v, s_all = ctx["v"], ctx.pop("s_all")
        slabs = []
        for sl in range(vw // LANES):
            vs = v[:, sl * LANES:(sl + 1) * LANES]
            acc = None
            for h, v_h in ((2 * sl, jnp.where(head_a, vs, jnp.zeros_like(vs))),
                           (2 * sl + 1, jnp.where(head_a, jnp.zeros_like(vs), vs))):
                s_h = s_all[h * R:(h + 1) * R, :].astype(BF16)
                s_h = jnp.where(intra, s_h, jnp.zeros_like(s_h))
                part = _dot(s_h, v_h)
                acc = part if acc is None else acc + part
            slabs.append(acc)
        ctx["o"] = jnp.concatenate(slabs, axis=1)

    def inter_out():
        v, q_t, k_end, decay = ctx.pop("v"), ctx.pop("q_t"), ctx.pop("k_end"), ctx.pop("decay")
        st = [state_ref[r0:r1, :] for _, r0, r1 in tiles]
        parts = []
        for ci in range(R // C):
            cs = slice(ci * C, (ci + 1) * C)
            row_parts = []
            for t, (l0, r0, r1) in enumerate(tiles):
                row_parts.append(_dot_nt(q_t[cs, l0:l0 + LANES], st[t].astype(BF16)))
                kv_t = _dot_tn(v[cs, r0:r1], k_end[cs, l0:l0 + LANES])
                st[t] = st[t] * decay[ci][:, l0:l0 + LANES] + jnp.where(head_diag[t], kv_t, 0.0)
            parts.append(jnp.concatenate(row_parts, axis=1))
        for t, (_, r0, r1) in enumerate(tiles):
            state_ref[r0:r1, :] = st[t]
        ctx["o"] = ctx["o"] + jnp.concatenate(parts, axis=0)

    def finish():
        o = ctx.pop("o")
        r = x[:, r_off:r_off + vw]
        o2 = o * o
        ms = jnp.concatenate([_pair_sums(o2[:, sl * LANES:(sl + 1) * LANES], head_a)
                              for sl in range(vw // LANES)], axis=1) * (1.0 / GLA_DV)
        y = o * lax.rsqrt(ms + EPS) * gn
        o_ref[row0:row0 + R, :] = (y * (r * jax.nn.sigmoid(r))).astype(BF16)

    return prepare, scores, intra_out, inter_out, finish


def _gelu_tanh(x):
    c0 = math.sqrt(2.0 / math.pi)
    return 0.5 * x * (1.0 + jnp.tanh(c0 * (x + 0.044715 * (x * x * x))))


def _in_proj_gla_kernel(x_ref, g_ref, w_ref, lg_ref, lb_ref, sw_ref, sb_ref, qkg_ref, cos_ref, sin_ref,
                        wa_ref, ba_ref, gn_ref, att_ref, gla_ref, sg_ref, state_ref,
                        *, n_att, n_gla, n_groups, tiles_per_seq, gla_dims):
    T = SG_CHUNK
    width = sg_ref.shape[1]
    n_qk = qkg_ref.shape[1]
    half = ROPE_DIM // 2

    @pl.when(pl.program_id(0) % tiles_per_seq == 0)
    def _():
        state_ref[...] = jnp.zeros_like(state_ref)

    x = x_ref[...]
    ms = jnp.mean(x * x, axis=-1, keepdims=True)
    h = (x * lax.rsqrt(ms + EPS) * g_ref[...]).astype(BF16)
    lane = lax.broadcasted_iota(jnp.int32, (1, LANES), 1)
    head_a = lane < D_HEAD
    first_half = lane % D_HEAD < half

    def qk_epilogue(qk_wide, s0):
        xs = qk_wide[:, s0:s0 + LANES]
        y = xs * lax.rsqrt(_pair_sums(xs * xs, head_a) * (1.0 / D_HEAD) + EPS) * qkg_ref[:, s0:s0 + LANES]
        left = pltpu.roll(y, LANES - half, axis=1)
        right = pltpu.roll(y, half, axis=1)
        att_ref[:, s0:s0 + LANES] = (y * cos_ref[...] + jnp.where(first_half, left, right) * sin_ref[...]).astype(BF16)

    def sg_chunk(z, c):
        rows = slice(c * T, (c + 1) * T)
        row = lax.broadcasted_iota(jnp.int32, (T, n_groups * T), 0)
        col = lax.broadcasted_iota(jnp.int32, (T, n_groups * T), 1) % T
        w_cat = jnp.where(row >= col, sw_ref[...], 0.0).astype(BF16)
        lane_group = lax.broadcasted_iota(jnp.int32, (1, width), 1) // SG_GROUP_DIM
        u = _gelu_tanh(z[rows, :width])
        v = _gelu_tanh(z[rows, width:])
        mu = jnp.mean(v, axis=-1, keepdims=True)
        vc = v - mu
        var = jnp.mean(vc * vc, axis=-1, keepdims=True)
        vn = (vc * lax.rsqrt(var + EPS) * lg_ref[...] + lb_ref[...]).astype(BF16)
        stacked = jnp.concatenate([jnp.where(lane_group == g, vn, jnp.zeros_like(vn)) for g in range(n_groups)],
                                  axis=0)
        sg_ref[rows, :] = (u * (_dot(w_cat, stacked) + sb_ref[...])).astype(BF16)

    gla = _dot(h, w_ref[:, n_att:n_att + n_gla])
    blocks = [_gla_block_phases(gla[r0:r0 + GLA_ROWS, :], wa_ref[...], ba_ref[...], gn_ref[...], state_ref,
                                gla_ref, r0, **gla_dims) for r0 in range(0, x.shape[0], GLA_ROWS)]
    blocks[0][0]()
    qk_wide = _dot(h, w_ref[:, :n_qk])
    blocks[1][0]()
    z = _dot(h, w_ref[:, n_att + n_gla:])
    blocks[0][1]()
    blocks[1][1]()
    att_ref[:, n_qk:] = _dot(h, w_ref[:, n_qk:n_att]).astype(BF16)
    blocks[0][2]()
    for s0 in range(0, n_qk // 2, LANES):
        qk_epilogue(qk_wide, s0)
    blocks[1][2]()
    for s0 in range(n_qk // 2, n_qk, LANES):
        qk_epilogue(qk_wide, s0)
    blocks[0][3]()
    sg_chunk(z, 0)
    sg_chunk(z, 1)
    blocks[1][3]()
    sg_chunk(z, 2)
    sg_chunk(z, 3)
    blocks[0][4]()
    blocks[1][4]()


def _in_proj_gla(x2, g, w, lg, lb, sw, sb, qkg, cos_t, sin_t, wa, ba, gn, n_att, n_gla, n_groups, gla_dims):
    t, d = x2.shape
    width = lg.shape[1]
    seq = cos_t.shape[0]
    tm = ROW_TILE
    vw = gla_dims["vw"]
    assert tm == 2 * GLA_ROWS and tm == 4 * SG_CHUNK and seq % tm == 0
    pos_spec = pl.BlockSpec((tm, LANES), lambda i: (i % (seq // tm), 0))
    consts = (g, w, lg, lb, sw, sb, qkg)
    return pl.pallas_call(
        functools.partial(_in_proj_gla_kernel, n_att=n_att, n_gla=n_gla, n_groups=n_groups,
                          tiles_per_seq=seq // tm, gla_dims=gla_dims),
        name="in_proj_gla",
        grid=(t // tm,),
        in_specs=[pl.BlockSpec((tm, d), lambda i: (i, 0))] + [_const_spec(c.shape) for c in consts]
        + [pos_spec, pos_spec] + [_const_spec(c.shape) for c in (wa, ba, gn)],
        out_specs=[pl.BlockSpec((tm, n_att), lambda i: (i, 0)),
                   pl.BlockSpec((tm, vw), lambda i: (i, 0)),
                   pl.BlockSpec((tm, width), lambda i: (i, 0))],
        out_shape=[jax.ShapeDtypeStruct((t, n_att), BF16),
                   jax.ShapeDtypeStruct((t, vw), BF16),
                   jax.ShapeDtypeStruct((t, width), BF16)],
        scratch_shapes=[pltpu.VMEM((vw, LANES), F32)],
        compiler_params=pltpu.CompilerParams(dimension_semantics=("arbitrary",),
                                             vmem_limit_bytes=VMEM_LIMIT_BYTES),
    )(x2, *consts, cos_t, sin_t, wa, ba, gn)


def _att_kernel(q_ref, k_ref, v_ref, o_ref,
                st_q, st_k, st_v, mid_q, mid_k, mid_v,
                qa_n, qb_n, va_n, vb_n, qa_r, qb_r, k_r, va_r, vb_r, *res, seq):
    res_n, res_16, res_mid, res_nat = (res[3 * i:3 * i + 3] for i in range(4))
    L = ATT_BLOCK
    n4 = seq // 4
    n16 = seq // 16
    q4 = L // 4
    lane = lax.broadcasted_iota(jnp.int32, (1, LANES), 1)
    head_a = lane < D_HEAD
    ones_a = jnp.where(head_a, 1.0, 0.0).astype(BF16)
    ones_b = jnp.where(head_a, 0.0, 1.0).astype(BF16)

    def split_heads(x):
        return jnp.where(head_a, x, 0.0).astype(BF16), jnp.where(head_a, 0.0, x).astype(BF16)

    for c0 in range(0, seq, PREP_ROWS):
        rows = slice(c0, c0 + PREP_ROWS)
        q = q_ref[0, rows, :]
        v = v_ref[0, rows, :]
        zero = jnp.zeros_like(q)
        qa_n[rows, :] = jnp.where(head_a, q, zero)
        qb_n[rows, :] = jnp.where(head_a, zero, q)
        va_n[rows, :LANES] = jnp.where(head_a, v, zero)
        vb_n[rows, :LANES] = jnp.where(head_a, zero, v)
        st_q[rows, :] = q.astype(F32)
        st_k[rows, :] = k_ref[0, rows, :].astype(F32)
        st_v[rows, :] = v.astype(F32)
        for ref in (va_n, va_r):
            ref[rows, LANES:] = jnp.broadcast_to(ones_a, (PREP_ROWS, LANES))
        for ref in (vb_n, vb_r):
            ref[rows, LANES:] = jnp.broadcast_to(ones_b, (PREP_ROWS, LANES))

    def regroup(st, mid, emit):
        for rho in range(4):
            mid[rho * n4:(rho + 1) * n4, :] = st[pl.ds(rho, n4, stride=4), :]
        for rho in range(4):
            for c in range(4):
                emit((rho + 4 * c) * n16, mid[pl.ds(rho * n4 + c, n16, stride=4), :])

    def emit_q(r0, blk):
        qa_r[r0:r0 + n16, :], qb_r[r0:r0 + n16, :] = split_heads(blk)

    def emit_k(r0, blk):
        k_r[r0:r0 + n16, :] = blk.astype(BF16)

    def emit_v(r0, blk):
        va_r[r0:r0 + n16, :LANES], vb_r[r0:r0 + n16, :LANES] = split_heads(blk)

    regroup(st_q, mid_q, emit_q)
    regroup(st_k, mid_k, emit_k)
    regroup(st_v, mid_v, emit_v)

    def band_bias(dist):
        return jnp.where((dist >= 0) & (dist <= L), 0.0, MASK_VALUE).astype(F32)

    qi = lax.broadcasted_iota(jnp.int32, (L, L), 0)
    kj = lax.broadcasted_iota(jnp.int32, (L, L), 1)
    qi2 = lax.broadcasted_iota(jnp.int32, (L, 2 * L), 0)
    kj2 = lax.broadcasted_iota(jnp.int32, (L, 2 * L), 1)
    bias_first = band_bias(qi - kj)
    bias_win = band_bias(qi2 + L - kj2)

    def pos4(u):
        return 4 * (u % q4) + u // q4

    bias4_first = band_bias(pos4(qi) - pos4(kj))
    bias4_win = band_bias(pos4(qi2) + L - (pos4(kj2 % L) + L * (kj2 // L)))

    def put(dst, rows, res):
        for ref, val in zip(dst, res):
            ref[rows, :] = val

    def att_core(qa, qb, kk, va, vb, bias):
        sa = _dot_nt(qa, kk) + bias
        sb = _dot_nt(qb, kk) + bias
        ma = jnp.max(sa, axis=-1, keepdims=True)
        mb = jnp.max(sb, axis=-1, keepdims=True)
        pa = jnp.exp2(sa - ma).astype(BF16)
        pb = jnp.exp2(sb - mb).astype(BF16)
        acc = _dot(pa, va) + _dot(pb, vb)
        return acc[:, :LANES], acc[:, LANES:], jnp.where(head_a, ma, mb)

    for j in range(seq // L):
        k0 = max(j - 1, 0) * L
        ks = slice(k0, (j + 1) * L)
        qs = slice(j * L, (j + 1) * L)
        put(res_n, qs, att_core(qa_n[qs, :], qb_n[qs, :], k_ref[0, ks, :], va_n[ks, :], vb_n[ks, :],
                                bias_first if j == 0 else bias_win))

    for r in range(16):
        rs = slice(r * n16, (r + 1) * n16)
        put(res_16, rs, att_core(qa_r[rs, :], qb_r[rs, :], k_r[rs, :], va_r[rs, :], vb_r[rs, :], bias_first))

    def runs(ref, rho, j):
        return [ref[(rho + 4 * c) * n16 + j * q4:(rho + 4 * c) * n16 + (j + 1) * q4, :] for c in range(4)]

    def gather4(ref, rho, j0, j1):
        return jnp.concatenate([blk for j in range(j0, j1 + 1) for blk in runs(ref, rho, j)], axis=0)

    def combine(a, b):
        (acc_a, den_a, m_a), (acc_b, den_b, m_b) = a, b
        m = jnp.maximum(m_a, m_b)
        wa = jnp.exp2(m_a - m)
        wb = jnp.exp2(m_b - m)
        return wa * acc_a + wb * acc_b, wa * den_a + wb * den_b, m

    def get(src, rows):
        return [ref[rows, :] for ref in src]

    for rho in range(4):
        for j in range(n4 // L):
            j0 = max(j - 1, 0)
            res = att_core(gather4(qa_r, rho, j, j), gather4(qb_r, rho, j, j), gather4(k_r, rho, j0, j),
                           gather4(va_r, rho, j0, j), gather4(vb_r, rho, j0, j),
                           bias4_first if j == 0 else bias4_win)
            for c in range(4):
                dst = slice((rho + 4 * c) * n16 + j * q4, (rho + 4 * c) * n16 + (j + 1) * q4)
                put(res_16, dst, combine([val[c * q4:(c + 1) * q4, :] for val in res], get(res_16, dst)))

    for src, mid, dst in zip(res_16, res_mid, res_nat):
        for rho in range(4):
            for c in range(4):
                r0 = (rho + 4 * c) * n16
                mid[pl.ds(rho * n4 + c, n16, stride=4), :] = src[r0:r0 + n16, :]
        for rho in range(4):
            dst[pl.ds(rho, n4, stride=4), :] = mid[rho * n4:(rho + 1) * n4, :]

    for c0 in range(0, seq, COMBINE_ROWS):
        rows = slice(c0, c0 + COMBINE_ROWS)
        acc, den, _ = combine(get(res_n, rows), get(res_nat, rows))
        o_ref[0, rows, :] = (acc / den).astype(BF16)


def _attention(att, batch, seq):
    n_slab = att.shape[-1] // 3 // LANES
    att3 = att.reshape(batch, seq, att.shape[-1])

    def slab_spec(off):
        return pl.BlockSpec((1, seq, LANES), lambda b, h: (b, 0, off + h))

    rows_f32 = pltpu.VMEM((seq, LANES), F32)
    rows_bf16 = pltpu.VMEM((seq, LANES), BF16)
    rows2_bf16 = pltpu.VMEM((seq, 2 * LANES), BF16)
    return pl.pallas_call(
        functools.partial(_att_kernel, seq=seq),
        name="dilated_attention",
        grid=(batch, n_slab),
        in_specs=[slab_spec(0), slab_spec(n_slab), slab_spec(2 * n_slab)],
        out_specs=pl.BlockSpec((1, seq, LANES), lambda b, h: (b, 0, h)),
        out_shape=jax.ShapeDtypeStruct((batch, seq, n_slab * LANES), BF16),
        scratch_shapes=[rows_f32] * 6
        + [rows_bf16, rows_bf16, rows2_bf16, rows2_bf16]
        + [rows_bf16, rows_bf16, rows_bf16, rows2_bf16, rows2_bf16]
        + [rows_f32] * 12,
        compiler_params=pltpu.CompilerParams(dimension_semantics=("parallel", "parallel"),
                                             vmem_limit_bytes=VMEM_LIMIT_BYTES),
    )(att3, att3, att3)


def _out_ffn_kernel(x_ref, oa_ref, og_ref, os_ref, wo_ref, g_ref, w1_ref, w2_ref, y_ref, *, n_a, n_g):
    mix = (_dot(oa_ref[...], wo_ref[:n_a, :]) + _dot(og_ref[...], wo_ref[n_a:n_a + n_g, :])
           + _dot(os_ref[...], wo_ref[n_a + n_g:, :]))
    x1 = x_ref[...] + mix
    ms = jnp.mean(x1 * x1, axis=-1, keepdims=True)
    h = (x1 * lax.rsqrt(ms + EPS) * g_ref[...]).astype(BF16)
    acc = x1
    d_ff = w1_ref.shape[1]
    for c0 in range(0, d_ff, FF_CHUNK):
        a = jnp.maximum(_dot(h, w1_ref[:, c0:c0 + FF_CHUNK]), 0.0)
        acc = acc + _dot((a * a).astype(BF16), w2_ref[c0:c0 + FF_CHUNK, :])
    y_ref[...] = acc


def _out_ffn(x2, oa, og, os_, wo, g2, w1, w2):
    t, d = x2.shape
    tm = ROW_TILE
    n_a, n_g, n_s = oa.shape[-1], og.shape[-1], os_.shape[-1]
    assert w1.shape[1] % FF_CHUNK == 0
    row_spec = lambda n: pl.BlockSpec((tm, n), lambda i: (i, 0))
    resident = lambda a: pl.BlockSpec(a.shape, lambda i: (0, 0), pipeline_mode=pl.Buffered(1))
    return pl.pallas_call(
        functools.partial(_out_ffn_kernel, n_a=n_a, n_g=n_g),
        name="out_proj_mlp",
        grid=(t // tm,),
        in_specs=[row_spec(d), row_spec(n_a), row_spec(n_g), row_spec(n_s),
                  resident(wo), _const_spec((1, d)), resident(w1), resident(w2)],
        out_specs=row_spec(d),
        out_shape=jax.ShapeDtypeStruct((t, d), F32),
        compiler_params=pltpu.CompilerParams(dimension_semantics=("parallel",),
                                             vmem_limit_bytes=VMEM_LIMIT_BYTES),
    )(x2, oa.reshape(t, n_a), og.reshape(t, n_g), os_.reshape(t, n_s), wo, g2, w1, w2)


def _pad_cols(w, n):
    return jnp.pad(w, ((0, 0), (0, n - w.shape[1])))


def _rope_tables(seq):
    inv = ROPE_THETA ** (-jnp.arange(0, ROPE_DIM, 2, dtype=F32) / ROPE_DIM)
    ang = jnp.arange(seq, dtype=F32)[:, None] * inv[None, :]
    pad = D_HEAD - ROPE_DIM
    cos_h = jnp.concatenate([jnp.cos(ang), jnp.cos(ang), jnp.ones((seq, pad), F32)], axis=1)
    sin_h = jnp.concatenate([-jnp.sin(ang), jnp.sin(ang), jnp.zeros((seq, pad), F32)], axis=1)
    return jnp.tile(cos_h, (1, LANES // D_HEAD)), jnp.tile(sin_h, (1, LANES // D_HEAD))


def kernel(x, norm1_g, w_in, q_norm_g, k_norm_g, gla_w_a2, gla_b_a, gla_norm_g, sg_ln_g, sg_ln_b, sg_w, sg_b,
           w_out, norm2_g, w_ff1, w_ff2):
    batch, seq, d_model = x.shape
    depth = w_in.shape[0]
    att_dim = 3 * d_model // 8
    gla_vdim = 3 * d_model // 8
    gla_heads = gla_vdim // GLA_DV
    gla_kdim = gla_heads * GLA_DK
    gla_rank = gla_w_a2.shape[1]
    sg_dim = d_model - att_dim - gla_vdim
    sg_groups = sg_dim // SG_GROUP_DIM
    kw = -(-gla_kdim // LANES) * LANES
    n_att = 3 * att_dim
    n_gla = 2 * kw + 2 * gla_vdim
    assert gla_kdim + gla_rank <= kw
    assert att_dim % LANES == 0 and gla_vdim % LANES == 0 and sg_dim % LANES == 0
    assert seq // 16 == ATT_BLOCK and seq % PREP_ROWS == 0

    cos_t, sin_t = _rope_tables(seq)
    x2 = x.reshape(batch * seq, d_model)
    splits = [att_dim, att_dim, att_dim, gla_kdim, gla_kdim, gla_vdim, gla_vdim, gla_rank, 2 * sg_dim]
    offs = [0]
    for s in splits:
        offs.append(offs[-1] + s)

    for l in range(depth):
        w = w_in[l]
        seg = [w[:, offs[i]:offs[i + 1]] for i in range(len(splits))]
        w_packed = jnp.concatenate(
            [seg[0], seg[1], seg[2], _pad_cols(jnp.concatenate([seg[3], seg[7]], axis=1), kw),
             _pad_cols(seg[4], kw), seg[5], seg[6], seg[8]], axis=1).astype(BF16)
        w_cat = jnp.transpose(sg_w[l], (1, 0, 2)).reshape(SG_CHUNK, sg_groups * SG_CHUNK)
        sg_bias = jnp.repeat(sg_b[l].T, SG_GROUP_DIM, axis=1)
        att_heads = att_dim // D_HEAD
        qkg = jnp.concatenate([jnp.tile(q_norm_g[l], att_heads) * (D_HEAD ** -0.5 * math.log2(math.e)),
                               jnp.tile(k_norm_g[l], att_heads)])[None, :]
        wa = jnp.pad(gla_w_a2[l], ((gla_kdim, kw - gla_kdim - gla_rank), (0, kw - gla_kdim))).astype(BF16)
        ba = jnp.pad(gla_b_a[l], (0, kw - gla_kdim))[None, :]
        gn = jnp.tile(gla_norm_g[l], gla_heads)[None, :]
        att, o_gla, o_sg = _in_proj_gla(x2, norm1_g[l][None, :], w_packed, sg_ln_g[l][None, :],
                                        sg_ln_b[l][None, :], w_cat, sg_bias, qkg, cos_t, sin_t, wa, ba, gn,
                                        n_att, n_gla, sg_groups, dict(n_heads=gla_heads, kw=kw, vw=gla_vdim))

        o_att = _attention(att, batch, seq)

        x2 = _out_ffn(x2, o_att, o_gla, o_sg, w_out[l].astype(BF16), norm2_g[l][None, :],
                      w_ff1[l].astype(BF16), w_ff2[l].astype(BF16))
    return x2.reshape(batch, seq, d_model)
```
